```python
import math
import jax, jax.numpy as jnp
from jax import lax
import numpy as np

D_MODEL = 2048
BATCH = 4
SEQ = 2048
DEPTH = 4

N_MEM = 256
BRANCH_W = 1024
N_BRANCH = 3
CHUNK = 128
SG_GROUPS = 8
SG_GROUP_W = BRANCH_W // SG_GROUPS
SSM_GROUP_W = 16
SSM_GROUPS = BRANCH_W // SSM_GROUP_W
SSM_STATE = 64
XA_HEADS = 4
XA_HEAD_DIM = BRANCH_W // XA_HEADS
N_IN = 7 * BRANCH_W + N_BRANCH * D_MODEL
EPS = 1e-6
DT_MIN = 1e-3
DT_MAX = 1e-1
EIG_CLIP = 1e-4

kernel_name = 'hybrid_sg_s5_memxattn_trunk'


def rms_norm(x, g):
    xf = x.astype(jnp.float32)
    y = xf * lax.rsqrt(jnp.mean(xf * xf, axis=-1, keepdims=True) + EPS)
    return (y * g.astype(jnp.float32)).astype(x.dtype)


def layer_norm(x, g, b):
    xf = x.astype(jnp.float32)
    mu = jnp.mean(xf, axis=-1, keepdims=True)
    xc = xf - mu
    y = xc * lax.rsqrt(jnp.mean(xc * xc, axis=-1, keepdims=True) + EPS)
    return (y * g.astype(jnp.float32) + b.astype(jnp.float32)).astype(x.dtype)


def spatial_gating(u, v, ln_g, ln_b, w_s, b_s):
    bsz, seq, _ = v.shape
    n_chunks = seq // CHUNK
    vn = layer_norm(v, ln_g, ln_b).reshape(bsz, n_chunks, CHUNK, SG_GROUPS, SG_GROUP_W)
    causal = jnp.tril(jnp.ones((CHUNK, CHUNK), dtype=bool))
    w = jnp.where(causal[None], w_s, 0).astype(vn.dtype)
    z = jnp.einsum('gts,bcsgd->bctgd', w, vn) + b_s.T[None, None, :, :, None].astype(vn.dtype)
    return u * z.reshape(bsz, seq, BRANCH_W)


def s5_branch(u, a_re, a_im, log_dt, b_re, b_im, c_re, c_im, d, glu_w, glu_b):
    bsz, seq, _ = u.shape
    f32 = jnp.float32
    uf = u.astype(f32).reshape(bsz, seq, SSM_GROUPS, SSM_GROUP_W)
    lam = lax.complex(jnp.minimum(a_re.astype(f32), -EIG_CLIP), a_im.astype(f32))
    dt = jnp.exp(log_dt.astype(f32))[:, None]
    lam_bar = jnp.exp(lam * dt)
    b_mat = lax.complex(b_re.astype(f32), b_im.astype(f32))
    b_bar = ((lam_bar - 1) / lam)[:, :, None] * b_mat
    c_mat = lax.complex(c_re.astype(f32), c_im.astype(f32))
    bu = jnp.einsum('bsgh,gph->bsgp', uf.astype(jnp.complex64), b_bar)
    a_el = jnp.broadcast_to(lam_bar, bu.shape)

    def combine(left, right):
        a_l, s_l = left
        a_r, s_r = right
        return a_r * a_l, a_r * s_l + s_r

    _, states = lax.associative_scan(combine, (a_el, bu), axis=1)
    y = jnp.einsum('bsgp,ghp->bsgh', states, c_mat).real + d.astype(f32) * uf
    y = jax.nn.gelu(y.reshape(bsz, seq, BRANCH_W)).astype(u.dtype)
    return y * jax.nn.sigmoid(y @ glu_w + glu_b)


def memory_cross_attention(q, mem_n, w_k, w_v):
    bsz, seq, _ = q.shape
    n_mem = mem_n.shape[1]
    qh = q.reshape(bsz, seq, XA_HEADS, XA_HEAD_DIM)
    k = (mem_n @ w_k).reshape(bsz, n_mem, XA_HEADS, XA_HEAD_DIM)
    v = (mem_n @ w_v).reshape(bsz, n_mem, XA_HEADS, XA_HEAD_DIM)
    scores = jnp.einsum('bshd,bmhd->bhsm', qh, k).astype(jnp.float32) * (XA_HEAD_DIM ** -0.5)
    p = jax.nn.softmax(scores, axis=-1).astype(v.dtype)
    o = jnp.einsum('bhsm,bmhd->bshd', p, v)
    return o.reshape(bsz, seq, BRANCH_W)


def hybrid_layer(x, mem, pre_g, post_g, mem_g, w_in, sg_ln_g, sg_ln_b, sg_w, sg_b,
                 a_re, a_im, log_dt, b_re, b_im, c_re, c_im, d, glu_w, glu_b,
                 xa_wk, xa_wv, w_branch, w_out):
    bsz, seq, _ = x.shape
    h = rms_norm(x, pre_g)
    proj = h @ w_in
    splits = [BRANCH_W * i for i in range(1, 8)]
    a_u, a_v, a_g, b_x, b_g, x_q, x_g, gate_logits = jnp.split(proj, splits, axis=-1)
    y_a = spatial_gating(a_u, a_v, sg_ln_g, sg_ln_b, sg_w, sg_b) * jax.nn.silu(a_g)
    y_b = s5_branch(b_x, a_re, a_im, log_dt, b_re, b_im, c_re, c_im, d, glu_w, glu_b) * jax.nn.silu(b_g)
    y_x = memory_cross_attention(x_q, rms_norm(mem, mem_g), xa_wk, xa_wv) * jax.nn.silu(x_g)
    ys = jnp.stack([y_a, y_b, y_x], axis=2)
    branch = jnp.einsum('bsnc,ncd->bsnd', ys, w_branch)
    gates = jax.nn.sigmoid(gate_logits.reshape(bsz, seq, N_BRANCH, D_MODEL))
    merged = jnp.sum(gates * branch, axis=2)
    out = merged @ w_out
    return x + rms_norm(out, post_g)


def setup_inputs(seed: int = 0) -> dict:
    key = jax.random.key(seed)
    ks = jax.random.split(key, 26)
    f32 = jnp.float32

    def nrm(k, shape, scale):
        return jax.random.normal(k, shape, f32) * scale

    L, D, W = DEPTH, D_MODEL, BRANCH_W
    G, H, P = SSM_GROUPS, SSM_GROUP_W, SSM_STATE
    x = nrm(ks[0], (BATCH, SEQ, D), 1.0)
    mem = nrm(ks[1], (BATCH, N_MEM, D), 1.0)
    pre_norm_g = 1.0 + nrm(ks[2], (L, D), 0.05)
    post_norm_g = 1.0 + nrm(ks[3], (L, D), 0.05)
    mem_norm_g = 1.0 + nrm(ks[4], (L, D), 0.05)
    w_in = nrm(ks[5], (L, D, N_IN), D ** -0.5)
    sg_ln_g = 1.0 + nrm(ks[6], (L, W), 0.05)
    sg_ln_b = nrm(ks[7], (L, W), 0.01)
    sg_w = nrm(ks[8], (L, SG_GROUPS, CHUNK, CHUNK), 0.5 * CHUNK ** -0.5)
    sg_b = 1.0 + nrm(ks[9], (L, SG_GROUPS, CHUNK), 0.1)
    ssm_a_re = -0.5 + nrm(ks[10], (L, G, P), 0.01)
    ssm_a_im = jnp.broadcast_to(math.pi * jnp.arange(P, dtype=f32), (L, G, P))
    ssm_log_dt = jax.random.uniform(ks[11], (L, G), f32, math.log(DT_MIN), math.log(DT_MAX))
    ssm_b_re = nrm(ks[12], (L, G, P, H), (2 * H) ** -0.5)
    ssm_b_im = nrm(ks[13], (L, G, P, H), (2 * H) ** -0.5)
    ssm_c_re = nrm(ks[14], (L, G, H, P), P ** -0.5)
    ssm_c_im = nrm(ks[15], (L, G, H, P), P ** -0.5)
    ssm_d = nrm(ks[16], (L, G, H), 0.5)
    glu_w = nrm(ks[17], (L, W, W), W ** -0.5)
    glu_b = nrm(ks[18], (L, W), 0.01)
    xa_wk = nrm(ks[19], (L, D, W), D ** -0.5)
    xa_wv = nrm(ks[20], (L, D, W), D ** -0.5)
    w_branch = nrm(ks[21], (L, N_BRANCH, W, D), W ** -0.5)
    w_out = nrm(ks[22], (L, D, D), D ** -0.5)
    return {'x': x, 'mem': mem, 'pre_norm_g': pre_norm_g, 'post_norm_g': post_norm_g,
            'mem_norm_g': mem_norm_g, 'w_in': w_in, 'sg_ln_g': sg_ln_g, 'sg_ln_b': sg_ln_b,
            'sg_w': sg_w, 'sg_b': sg_b, 'ssm_a_re': ssm_a_re, 'ssm_a_im': ssm_a_im,
            'ssm_log_dt': ssm_log_dt, 'ssm_b_re': ssm_b_re, 'ssm_b_im': ssm_b_im,
            'ssm_c_re': ssm_c_re, 'ssm_c_im': ssm_c_im, 'ssm_d': ssm_d, 'glu_w': glu_w,
            'glu_b': glu_b, 'xa_wk': xa_wk, 'xa_wv': xa_wv, 'w_branch': w_branch, 'w_out': w_out}


def reference(x, mem, pre_norm_g, post_norm_g, mem_norm_g, w_in, sg_ln_g, sg_ln_b, sg_w, sg_b,
              ssm_a_re, ssm_a_im, ssm_log_dt, ssm_b_re, ssm_b_im, ssm_c_re, ssm_c_im, ssm_d,
              glu_w, glu_b, xa_wk, xa_wv, w_branch, w_out):
    for l in range(DEPTH):
        x = hybrid_layer(x, mem, pre_norm_g[l], post_norm_g[l], mem_norm_g[l], w_in[l],
                         sg_ln_g[l], sg_ln_b[l], sg_w[l], sg_b[l],
                         ssm_a_re[l], ssm_a_im[l], ssm_log_dt[l], ssm_b_re[l], ssm_b_im[l],
                         ssm_c_re[l], ssm_c_im[l], ssm_d[l], glu_w[l], glu_b[l],
                         xa_wk[l], xa_wv[l], w_branch[l], w_out[l])
    return x
```

```python
import functools
import math

import jax
import jax.numpy as jnp
from jax import lax
from jax.experimental import pallas as pl
from jax.experimental.pallas import tpu as pltpu

D_MODEL = 2048
BATCH = 4
SEQ = 2048
DEPTH = 4
N_MEM = 256
BRANCH_W = 1024
N_BRANCH = 3
CHUNK = 128
SG_GROUPS = 8
SSM_GROUP_W = 16
SSM_GROUPS = BRANCH_W // SSM_GROUP_W
SSM_STATE = 64
XA_HEADS = 4
XA_HEAD_DIM = BRANCH_W // XA_HEADS
N_IN = 7 * BRANCH_W + N_BRANCH * D_MODEL
EPS = 1e-6
EIG_CLIP = 1e-4

TOKENS = BATCH * SEQ
F32 = jnp.float32
BF16 = jnp.bfloat16

V7X_VMEM_LIMIT_BYTES = 56 * 1024 * 1024

COL_A_U, COL_A_V, COL_A_G, COL_B_X, COL_B_G, COL_X_Q, COL_X_G, COL_GATES = range(8)

SSM_HALF_GROUPS = SSM_GROUPS // 2
SSM_LANE_BLOCKS = 2
SSM_BLOCK_GROUPS = SSM_HALF_GROUPS // SSM_LANE_BLOCKS
SSM_BLOCK_FEATS = SSM_BLOCK_GROUPS * SSM_GROUP_W
SSM_BLOCK_STATES = SSM_BLOCK_GROUPS * SSM_STATE
SSM_HALF_FEATS = SSM_HALF_GROUPS * SSM_GROUP_W
SSM_ROWS = 2 * BATCH
SSM_TIME_BLOCK = 128


def _params(*sem):
    return pltpu.CompilerParams(dimension_semantics=sem, vmem_limit_bytes=V7X_VMEM_LIMIT_BYTES)


def _sigmoid(x):
    return 1.0 / (1.0 + jnp.exp(-x))


def _silu(x):
    return x * _sigmoid(x)


def _gelu_tanh(x):
    c = math.sqrt(2.0 / math.pi)
    return 0.5 * x * (1.0 + jnp.tanh(c * (x + 0.044715 * (x * x * x))))


def _rmsnorm_kernel(x_ref, g_ref, o_ref):
    x = x_ref[...]
    ms = jnp.mean(x * x, axis=-1, keepdims=True)
    o_ref[...] = (x * lax.rsqrt(ms + EPS) * g_ref[...]).astype(o_ref.dtype)


def _rmsnorm(x2d, g, tm=512):
    m, d = x2d.shape
    return pl.pallas_call(
        _rmsnorm_kernel,
        grid=(m // tm,),
        in_specs=[pl.BlockSpec((tm, d), lambda i: (i, 0)),
                  pl.BlockSpec((1, d), lambda i: (0, 0))],
        out_specs=pl.BlockSpec((tm, d), lambda i: (i, 0)),
        out_shape=jax.ShapeDtypeStruct((m, d), BF16),
        compiler_params=_params("parallel"),
        name="rmsnorm",
    )(x2d, g.reshape(1, d))


def _mm_kernel(a_ref, w_ref, o_ref):
    o_ref[...] = jnp.dot(a_ref[...], w_ref[...], preferred_element_type=F32).astype(o_ref.dtype)


def _matmul(a, w, out_dtype, tm=1024, tn=1024):
    m, k = a.shape
    _, n = w.shape
    return pl.pallas_call(
        _mm_kernel,
        grid=(m // tm, n // tn),
        in_specs=[pl.BlockSpec((tm, k), lambda i, j: (i, 0)),
                  pl.BlockSpec((k, tn), lambda i, j: (0, j))],
        out_specs=pl.BlockSpec((tm, tn), lambda i, j: (i, j)),
        out_shape=jax.ShapeDtypeStruct((m, n), out_dtype),
        compiler_params=_params("parallel", "parallel"),
        name="matmul",
    )(a, w)


def _sg_kernel(u_ref, v_ref, g_ref, lng_ref, lnb_ref, w_ref, bias_ref, o_ref, *, n_chunks):
    v = v_ref[...].astype(F32)
    mu = jnp.mean(v, axis=-1, keepdims=True)
    vc = v - mu
    var = jnp.mean(vc * vc, axis=-1, keepdims=True)
    vn = (vc * lax.rsqrt(var + EPS) * lng_ref[...] + lnb_ref[...]).astype(BF16)
    row = lax.broadcasted_iota(jnp.int32, (CHUNK, CHUNK), 0)
    col = lax.broadcasted_iota(jnp.int32, (CHUNK, CHUNK), 1)
    causal = col <= row
    for grp in range(SG_GROUPS):
        w = jnp.where(causal, w_ref[grp], 0.0).astype(BF16)
        lanes = slice(grp * CHUNK, (grp + 1) * CHUNK)
        for c in range(n_chunks):
            rows = slice(c * CHUNK, (c + 1) * CHUNK)
            z = jnp.dot(w, vn[rows, lanes], preferred_element_type=F32) + bias_ref[:, lanes]
            gate = g_ref[rows, lanes].astype(F32)
            y = u_ref[rows, lanes].astype(F32) * z * _silu(gate)
            o_ref[rows, lanes] = y.astype(o_ref.dtype)


def _spatial_gating(proj, ln_g, ln_b, w_s, bias_full, tr=512):
    n_chunks = tr // CHUNK
    w = BRANCH_W
    return pl.pallas_call(
        functools.partial(_sg_kernel, n_chunks=n_chunks),
        grid=(TOKENS // tr,),
        in_specs=[pl.BlockSpec((tr, w), lambda i: (i, COL_A_U)),
                  pl.BlockSpec((tr, w), lambda i: (i, COL_A_V)),
                  pl.BlockSpec((tr, w), lambda i: (i, COL_A_G)),
                  pl.BlockSpec((1, w), lambda i: (0, 0)),
                  pl.BlockSpec((1, w), lambda i: (0, 0)),
                  pl.BlockSpec((SG_GROUPS, CHUNK, CHUNK), lambda i: (0, 0, 0)),
                  pl.BlockSpec((CHUNK, w), lambda i: (0, 0))],
        out_specs=pl.BlockSpec((tr, w), lambda i: (i, 0)),
        out_shape=jax.ShapeDtypeStruct((TOKENS, w), BF16),
        compiler_params=_params("parallel"),
        name="spatial_gating",
    )(proj, proj, proj, ln_g.reshape(1, w), ln_b.reshape(1, w), w_s, bias_full)


def _s5_kernel(u_ref, wbu_ref, wc_ref, lre_ref, lim_ref, d_ref, y_ref, st_ref, buf_ref, *, tt):
    rows = tt * SSM_ROWS
    ns = SSM_BLOCK_STATES
    nf = SSM_BLOCK_FEATS

    @pl.when(pl.program_id(0) == 0)
    def _():
        st_ref[...] = jnp.zeros_like(st_ref)

    u3 = u_ref[...]
    u = u3.reshape(rows, SSM_HALF_FEATS)
    half0 = (lax.broadcasted_iota(jnp.int32, (rows, nf), 0) & 1) == 0
    y_blocks = []
    for k in range(SSM_LANE_BLOCKS):
        uk = u[:, k * nf:(k + 1) * nf]
        lhs = jnp.concatenate([jnp.where(half0, uk, 0.0), jnp.where(half0, 0.0, uk)], axis=1)
        buf_ref[...] = jnp.dot(lhs.astype(BF16), wbu_ref[k], preferred_element_type=F32)
        lr = lre_ref[k]
        li = lim_ref[k]

        def step(t, carry):
            sr, si = carry
            r0 = pl.multiple_of(t * SSM_ROWS, SSM_ROWS)
            nr = lr * sr - li * si + buf_ref[pl.ds(r0, SSM_ROWS), 0:ns]
            ni = lr * si + li * sr + buf_ref[pl.ds(r0, SSM_ROWS), ns:2 * ns]
            buf_ref[pl.ds(r0, SSM_ROWS), 0:ns] = nr
            buf_ref[pl.ds(r0, SSM_ROWS), ns:2 * ns] = ni
            return nr, ni

        sr, si = lax.fori_loop(0, tt, step, (st_ref[k, 0], st_ref[k, 1]), unroll=8)
        st_ref[k, 0] = sr
        st_ref[k, 1] = si
        r = jnp.dot(buf_ref[...].astype(BF16), wc_ref[k], preferred_element_type=F32)
        y_blocks.append(jnp.where(half0, r[:, 0:nf], r[:, nf:2 * nf]))
    y = jnp.concatenate(y_blocks, axis=1).reshape(tt, SSM_ROWS, SSM_HALF_FEATS)
    y = y + d_ref[...][None] * u3
    y_ref[...] = _gelu_tanh(y)


def _s5_scan(u_r, wbu, wc, lre, lim, d8, tt=SSM_TIME_BLOCK):
    rows = tt * SSM_ROWS
    return pl.pallas_call(
        functools.partial(_s5_kernel, tt=tt),
        grid=(SEQ // tt,),
        in_specs=[pl.BlockSpec((tt, SSM_ROWS, SSM_HALF_FEATS), lambda i: (i, 0, 0)),
                  pl.BlockSpec(wbu.shape, lambda i: (0, 0, 0)),
                  pl.BlockSpec(wc.shape, lambda i: (0, 0, 0)),
                  pl.BlockSpec(lre.shape, lambda i: (0, 0, 0)),
                  pl.BlockSpec(lim.shape, lambda i: (0, 0, 0)),
                  pl.BlockSpec(d8.shape, lambda i: (0, 0))],
        out_specs=pl.BlockSpec((tt, SSM_ROWS, SSM_HALF_FEATS), lambda i: (i, 0, 0)),
        out_shape=jax.ShapeDtypeStruct((SEQ, SSM_ROWS, SSM_HALF_FEATS), F32),
        scratch_shapes=[pltpu.VMEM((SSM_LANE_BLOCKS, 2, SSM_ROWS, SSM_BLOCK_STATES), F32),
                        pltpu.VMEM((rows, 2 * SSM_BLOCK_STATES), F32)],
        compiler_params=_params("arbitrary"),
        name="s5_scan",
    )(u_r, wbu, wc, lre, lim, d8)


def _s5_discretise(a_re, a_im, log_dt, b_re, b_im, c_re, c_im, d):
    lam_re = jnp.minimum(a_re, -EIG_CLIP)
    lam_im = a_im
    dt = jnp.exp(log_dt)[:, None]
    mag = jnp.exp(lam_re * dt)
    lb_re = mag * jnp.cos(lam_im * dt)
    lb_im = mag * jnp.sin(lam_im * dt)
    den = lam_re * lam_re + lam_im * lam_im
    q_re = ((lb_re - 1) * lam_re + lb_im * lam_im) / den
    q_im = (lb_im * lam_re - (lb_re - 1) * lam_im) / den
    bbar_re = q_re[:, :, None] * b_re - q_im[:, :, None] * b_im
    bbar_im = q_re[:, :, None] * b_im + q_im[:, :, None] * b_re
    eye = jnp.eye(SSM_BLOCK_GROUPS, dtype=F32)
    nb, ng, p, h = SSM_LANE_BLOCKS, SSM_BLOCK_GROUPS, SSM_STATE, SSM_GROUP_W

    def bu_weight(x):
        x = x.reshape(2, nb, ng, p, h)
        return jnp.einsum('akiph,ij->kaihjp', x, eye).reshape(nb, 2 * ng * h, ng * p)

    wbu = jnp.concatenate([bu_weight(bbar_re), bu_weight(bbar_im)], axis=2)

    def c_weight(x):
        return jnp.einsum('akihp,ij->kipajh', x.reshape(2, nb, ng, h, p), eye).reshape(
            nb, ng * p, 2 * ng * h)

    wc = jnp.concatenate([c_weight(c_re), -c_weight(c_im)], axis=1)

    def lam_rows(x):
        x = x.reshape(2, nb, ng * p).transpose(1, 0, 2)
        return jnp.broadcast_to(x[:, None], (nb, BATCH, 2, ng * p)).reshape(nb, SSM_ROWS, ng * p)

    lre = lam_rows(lb_re)
    lim = lam_rows(lb_im)
    d8 = jnp.broadcast_to(d.reshape(1, 2, SSM_HALF_FEATS), (BATCH, 2, SSM_HALF_FEATS)).reshape(
        SSM_ROWS, SSM_HALF_FEATS)
    return wbu.astype(BF16), wc.astype(BF16), lre, lim, d8


def _glu_kernel(y_ref, bg_ref, w_ref, b_ref, o_ref):
    y = y_ref[...]
    z = jnp.dot(y, w_ref[...], preferred_element_type=F32) + b_ref[...]
    out = y.astype(F32) * _sigmoid(z) * _silu(bg_ref[...].astype(F32))
    o_ref[...] = out.astype(o_ref.dtype)


def _glu_gate(y_tok, proj, glu_w, glu_b, tm=1024):
    w = BRANCH_W
    return pl.pallas_call(
        _glu_kernel,
        grid=(TOKENS // tm,),
        in_specs=[pl.BlockSpec((tm, w), lambda i: (i, 0)),
                  pl.BlockSpec((tm, w), lambda i: (i, COL_B_G)),
                  pl.BlockSpec((w, w), lambda i: (0, 0)),
                  pl.BlockSpec((1, w), lambda i: (0, 0))],
        out_specs=pl.BlockSpec((tm, w), lambda i: (i, 0)),
        out_shape=jax.ShapeDtypeStruct((TOKENS, w), BF16),
        compiler_params=_params("parallel"),
        name="glu_gate",
    )(y_tok, proj, glu_w, glu_b.reshape(1, w))


def _xattn_kernel(q_ref, xg_ref, k_ref, v_ref, o_ref):
    scale = XA_HEAD_DIM ** -0.5
    for hd in range(XA_HEADS):
        lanes = slice(hd * XA_HEAD_DIM, (hd + 1) * XA_HEAD_DIM)
        s = lax.dot_general(q_ref[:, lanes], k_ref[:, lanes], (((1,), (1,)), ((), ())),
                            preferred_element_type=F32) * scale
        p = jnp.exp(s - jnp.max(s, axis=-1, keepdims=True))
        denom = jnp.sum(p, axis=-1, keepdims=True)
        o = jnp.dot(p.astype(BF16), v_ref[:, lanes], preferred_element_type=F32) / denom
        o = o * _silu(xg_ref[:, lanes].astype(F32))
        o_ref[:, lanes] = o.astype(o_ref.dtype)


def _cross_attention(proj, k, v, tq=1024):
    w = BRANCH_W
    per_b = SEQ // tq
    return pl.pallas_call(
        _xattn_kernel,
        grid=(TOKENS // tq,),
        in_specs=[pl.BlockSpec((tq, w), lambda i: (i, COL_X_Q)),
                  pl.BlockSpec((tq, w), lambda i: (i, COL_X_G)),
                  pl.BlockSpec((N_MEM, w), lambda i: (i // per_b, 0)),
                  pl.BlockSpec((N_MEM, w), lambda i: (i // per_b, 0))],
        out_specs=pl.BlockSpec((tq, w), lambda i: (i, 0)),
        out_shape=jax.ShapeDtypeStruct((TOKENS, w), BF16),
        compiler_params=_params("parallel"),
        name="cross_attention",
    )(proj, proj, k, v)


def _merge_kernel(ya_ref, yb_ref, yx_ref, ga_ref, gb_ref, gx_ref, wa_ref, wb_ref, wx_ref, o_ref):
    acc = _sigmoid(ga_ref[...].astype(F32)) * jnp.dot(
        ya_ref[...], wa_ref[...], preferred_element_type=F32)
    acc += _sigmoid(gb_ref[...].astype(F32)) * jnp.dot(
        yb_ref[...], wb_ref[...], preferred_element_type=F32)
    acc += _sigmoid(gx_ref[...].astype(F32)) * jnp.dot(
        yx_ref[...], wx_ref[...], preferred_element_type=F32)
    o_ref[...] = acc.astype(o_ref.dtype)


def _merge(y_a, y_b, y_x, proj, w_branch, tm=1024, tn=1024):
    w = BRANCH_W
    n_per_branch = D_MODEL // tn

    def y_spec():
        return pl.BlockSpec((tm, w), lambda n, i: (i, 0))

    def gate_spec(branch):
        return pl.BlockSpec((tm, tn), lambda n, i: (i, COL_GATES + branch * n_per_branch + n))

    def w_spec(branch):
        return pl.BlockSpec((None, w, tn), lambda n, i: (branch, 0, n))

    return pl.pallas_call(
        _merge_kernel,
        grid=(n_per_branch, TOKENS // tm),
        in_specs=[y_spec(), y_spec(), y_spec(),
                  gate_spec(0), gate_spec(1), gate_spec(2),
                  w_spec(0), w_spec(1), w_spec(2)],
        out_specs=pl.BlockSpec((tm, tn), lambda n, i: (i, n)),
        out_shape=jax.ShapeDtypeStruct((TOKENS, D_MODEL), BF16),
        compiler_params=_params("parallel", "parallel"),
        name="branch_merge",
    )(y_a, y_b, y_x, proj, proj, proj, w_branch, w_branch, w_branch)


def _out_kernel(m_ref, w_ref, x_ref, g_ref, o_ref):
    out = jnp.dot(m_ref[...], w_ref[...], preferred_element_type=F32)
    ms = jnp.mean(out * out, axis=-1, keepdims=True)
    o_ref[...] = x_ref[...] + out * lax.rsqrt(ms + EPS) * g_ref[...]


def _out_proj(merged, w_out, x2d, post_g, tm=512):
    d = D_MODEL
    return pl.pallas_call(
        _out_kernel,
        grid=(TOKENS // tm,),
        in_specs=[pl.BlockSpec((tm, d), lambda i: (i, 0)),
                  pl.BlockSpec((d, d), lambda i: (0, 0)),
                  pl.BlockSpec((tm, d), lambda i: (i, 0)),
                  pl.BlockSpec((1, d), lambda i: (0, 0))],
        out_specs=pl.BlockSpec((tm, d), lambda i: (i, 0)),
        out_shape=jax.ShapeDtypeStruct((TOKENS, d), F32),
        compiler_params=_params("parallel"),
        name="out_proj",
    )(merged, w_out, x2d, post_g.reshape(1, d))


def _to_scan_layout(proj):
    bx = proj[:, COL_B_X * BRANCH_W:(COL_B_X + 1) * BRANCH_W]
    bx = bx.reshape(BATCH, SEQ, 2, SSM_HALF_FEATS).transpose(1, 0, 2, 3)
    return bx.reshape(SEQ, SSM_ROWS, SSM_HALF_FEATS).astype(F32)


def _from_scan_layout(y_r):
    y = y_r.reshape(SEQ, BATCH, 2, SSM_HALF_FEATS).transpose(1, 0, 2, 3)
    return y.reshape(TOKENS, BRANCH_W).astype(BF16)


def kernel(x, mem, pre_norm_g, post_norm_g, mem_norm_g, w_in, sg_ln_g, sg_ln_b, sg_w, sg_b,
           ssm_a_re, ssm_a_im, ssm_log_dt, ssm_b_re, ssm_b_im, ssm_c_re, ssm_c_im, ssm_d,
           glu_w, glu_b, xa_wk, xa_wv, w_branch, w_out):
    x2d = x.reshape(TOKENS, D_MODEL)
    mem2d = mem.reshape(BATCH * N_MEM, D_MODEL)
    w_in_bf = w_in.astype(BF16)
    glu_w_bf = glu_w.astype(BF16)
    wk_bf = xa_wk.astype(BF16)
    wv_bf = xa_wv.astype(BF16)
    w_branch_bf = w_branch.astype(BF16)
    w_out_bf = w_out.astype(BF16)
    sg_bias = jnp.repeat(jnp.swapaxes(sg_b, 1, 2), CHUNK, axis=2)
    for l in range(DEPTH):
        h = _rmsnorm(x2d, pre_norm_g[l])
        proj = _matmul(h, w_in_bf[l], BF16)

        y_a = _spatial_gating(proj, sg_ln_g[l], sg_ln_b[l], sg_w[l], sg_bias[l])

        wbu, wc, lre, lim, d8 = _s5_discretise(
            ssm_a_re[l], ssm_a_im[l], ssm_log_dt[l], ssm_b_re[l], ssm_b_im[l],
            ssm_c_re[l], ssm_c_im[l], ssm_d[l])
        y_r = _s5_scan(_to_scan_layout(proj), wbu, wc, lre, lim, d8)
        y_b = _glu_gate(_from_scan_layout(y_r), proj, glu_w_bf[l], glu_b[l])

        mem_n = _rmsnorm(mem2d, mem_norm_g[l])
        k = _matmul(mem_n, wk_bf[l], BF16)
        v = _matmul(mem_n, wv_bf[l], BF16)
        y_x = _cross_attention(proj, k, v)

        merged = _merge(y_a, y_b, y_x, proj, w_branch_bf[l])
        x2d = _out_proj(merged, w_out_bf[l], x2d, post_norm_g[l])
    return x2d.reshape(BATCH, SEQ, D_MODEL)
```

```python
import functools
import math

import jax
import jax.numpy as jnp
from jax import lax
from jax.experimental import pallas as pl
from jax.experimental.pallas import tpu as pltpu

D_MODEL = 2048
BATCH = 4
SEQ = 2048
DEPTH = 4
N_MEM = 256
BRANCH_W = 1024
N_BRANCH = 3
CHUNK = 128
SG_GROUPS = 8
SSM_GROUP_W = 16
SSM_GROUPS = BRANCH_W // SSM_GROUP_W
SSM_STATE = 64
XA_HEADS = 4
XA_HEAD_DIM = BRANCH_W // XA_HEADS
N_IN = 7 * BRANCH_W + N_BRANCH * D_MODEL
EPS = 1e-6
EIG_CLIP = 1e-4

TOKENS = BATCH * SEQ
F32 = jnp.float32
BF16 = jnp.bfloat16

V7X_VMEM_LIMIT_BYTES = 56 * 1024 * 1024
LANES = 128
SUBLANES = 8

COL_A_U, COL_A_V, COL_A_G, COL_B_X, COL_B_G, COL_X_Q, COL_X_G, COL_GATES = range(8)

SSM_HALVES = 2
SSM_HALF_GROUPS = SSM_GROUPS // SSM_HALVES
SSM_LANE_BLOCKS = 2
SSM_BLOCK_GROUPS = SSM_HALF_GROUPS // SSM_LANE_BLOCKS
SSM_BLOCK_FEATS = SSM_BLOCK_GROUPS * SSM_GROUP_W
SSM_BLOCK_STATES = SSM_BLOCK_GROUPS * SSM_STATE
SSM_HALF_FEATS = SSM_HALF_GROUPS * SSM_GROUP_W
SSM_ROWS = SSM_HALVES * BATCH
SSM_STATE_SLABS = SSM_BLOCK_STATES // LANES
SSM_FEAT_SLABS = SSM_HALF_FEATS // LANES
SSM_TIME_BLOCK = 128


def _params(*sem):
    return pltpu.CompilerParams(dimension_semantics=sem, vmem_limit_bytes=V7X_VMEM_LIMIT_BYTES)


def _sigmoid(x):
    return 1.0 / (1.0 + jnp.exp(-x))


def _silu(x):
    return x * _sigmoid(x)


def _gelu_tanh(x):
    c = math.sqrt(2.0 / math.pi)
    return 0.5 * x * (1.0 + jnp.tanh(c * (x + 0.044715 * (x * x * x))))


def _rms(x, g):
    ms = jnp.mean(x * x, axis=-1, keepdims=True)
    return x * lax.rsqrt(ms + EPS) * g


def _layer_vec(arr, layer):
    n = arr.shape[-1]
    spec = pl.BlockSpec((None, 1, n), lambda *_: (layer, 0, 0))
    return arr.reshape(arr.shape[0], 1, n), spec


def _rmsnorm_kernel(x_ref, g_ref, o_ref):
    o_ref[...] = _rms(x_ref[...], g_ref[...]).astype(o_ref.dtype)


def _rmsnorm(x2d, g_all, layer, tm=512):
    m, d = x2d.shape
    g3, g_spec = _layer_vec(g_all, layer)
    return pl.pallas_call(
        _rmsnorm_kernel,
        grid=(m // tm,),
        in_specs=[pl.BlockSpec((tm, d), lambda i: (i, 0)), g_spec],
        out_specs=pl.BlockSpec((tm, d), lambda i: (i, 0)),
        out_shape=jax.ShapeDtypeStruct((m, d), BF16),
        compiler_params=_params("parallel"),
        name="rmsnorm",
    )(x2d, g3)


def _in_proj_kernel(a_ref, w_ref, o_ref, wbf_ref):
    @pl.when(pl.program_id(1) == 0)
    def _():
        wbf_ref[...] = w_ref[...].astype(BF16)

    o_ref[...] = jnp.dot(a_ref[...], wbf_ref[...], preferred_element_type=F32).astype(o_ref.dtype)


def _in_proj(h, w_in, layer, tm=1024, tn=1024):
    m, k = h.shape
    n = w_in.shape[-1]
    return pl.pallas_call(
        _in_proj_kernel,
        grid=(n // tn, m // tm),
        in_specs=[pl.BlockSpec((tm, k), lambda j, i: (i, 0)),
                  pl.BlockSpec((None, k, tn), lambda j, i: (layer, 0, j))],
        out_specs=pl.BlockSpec((tm, tn), lambda j, i: (i, j)),
        out_shape=jax.ShapeDtypeStruct((m, n), BF16),
        scratch_shapes=[pltpu.VMEM((k, tn), BF16)],
        compiler_params=_params("parallel", "arbitrary"),
        name="in_proj",
    )(h, w_in)


def _sg_kernel(u_ref, v_ref, g_ref, lng_ref, lnb_ref, w_ref, bias_ref, o_ref, *, n_chunks):
    v = v_ref[...].astype(F32)
    mu = jnp.mean(v, axis=-1, keepdims=True)
    vc = v - mu
    var = jnp.mean(vc * vc, axis=-1, keepdims=True)
    vn = (vc * lax.rsqrt(var + EPS) * lng_ref[...] + lnb_ref[...]).astype(BF16)
    row = lax.broadcasted_iota(jnp.int32, (CHUNK, CHUNK), 0)
    col = lax.broadcasted_iota(jnp.int32, (CHUNK, CHUNK), 1)
    causal = col <= row
    for grp in range(SG_GROUPS):
        w = jnp.where(causal, w_ref[grp], 0.0).astype(BF16)
        lanes = slice(grp * CHUNK, (grp + 1) * CHUNK)
        for c in range(n_chunks):
            rows = slice(c * CHUNK, (c + 1) * CHUNK)
            z = jnp.dot(w, vn[rows, lanes], preferred_element_type=F32) + bias_ref[:, lanes]
            gate = g_ref[rows, lanes].astype(F32)
            y = u_ref[rows, lanes].astype(F32) * z * _silu(gate)
            o_ref[rows, lanes] = y.astype(o_ref.dtype)


def _spatial_gating(proj, ln_g, ln_b, sg_w, sg_bias, layer, tr=512):
    n_chunks = tr // CHUNK
    w = BRANCH_W
    g3, g_spec = _layer_vec(ln_g, layer)
    b3, b_spec = _layer_vec(ln_b, layer)
    return pl.pallas_call(
        functools.partial(_sg_kernel, n_chunks=n_chunks),
        grid=(TOKENS // tr,),
        in_specs=[pl.BlockSpec((tr, w), lambda i: (i, COL_A_U)),
                  pl.BlockSpec((tr, w), lambda i: (i, COL_A_V)),
                  pl.BlockSpec((tr, w), lambda i: (i, COL_A_G)),
                  g_spec, b_spec,
                  pl.BlockSpec((None, SG_GROUPS, CHUNK, CHUNK), lambda i: (layer, 0, 0, 0)),
                  pl.BlockSpec((None, CHUNK, w), lambda i: (layer, 0, 0))],
        out_specs=pl.BlockSpec((tr, w), lambda i: (i, 0)),
        out_shape=jax.ShapeDtypeStruct((TOKENS, w), BF16),
        compiler_params=_params("parallel"),
        name="spatial_gating",
    )(proj, proj, proj, g3, b3, sg_w, sg_bias)


def _s5_kernel(bx_ref, bg_ref, wbu_ref, wc_ref, lre_ref, lim_ref, d_ref, gw_ref, gb_ref, o_ref,
               st_ref, u_scr, buf, y_scr, ytok, *, tt):
    nt = BATCH * tt
    ns = SSM_STATE_SLABS

    @pl.when(pl.program_id(0) == 0)
    def _():
        st_ref[...] = jnp.zeros_like(st_ref)

    for b in range(BATCH):
        for a in range(SSM_HALVES):
            for j in range(SSM_FEAT_SLABS):
                lane0 = a * SSM_HALF_FEATS + j * LANES
                u_scr[a, j, pl.ds(b, tt, stride=BATCH), :] = (
                    bx_ref[b, :, lane0:lane0 + LANES].astype(F32))

    for k in range(SSM_LANE_BLOCKS):
        slabs_per_block = SSM_BLOCK_FEATS // LANES

        def block_input(a):
            return jnp.concatenate(
                [u_scr[a, k * slabs_per_block + j] for j in range(slabs_per_block)], axis=1)

        for a in range(SSM_HALVES):
            bu = jnp.dot(block_input(a).astype(BF16), wbu_ref[a, k], preferred_element_type=F32)
            for s in range(2 * ns):
                buf[s, pl.ds(a, nt, stride=SSM_HALVES), :] = bu[:, s * LANES:(s + 1) * LANES]

        lam_re = [lre_ref[k, :, s * LANES:(s + 1) * LANES] for s in range(ns)]
        lam_im = [lim_ref[k, :, s * LANES:(s + 1) * LANES] for s in range(ns)]

        def step(t, carry):
            r0 = pl.multiple_of(t * SSM_ROWS, SSM_ROWS)
            new = []
            for s in range(ns):
                sr, si = carry[s], carry[ns + s]
                nr = lam_re[s] * sr - lam_im[s] * si + buf[s, pl.ds(r0, SSM_ROWS), :]
                ni = lam_re[s] * si + lam_im[s] * sr + buf[ns + s, pl.ds(r0, SSM_ROWS), :]
                buf[s, pl.ds(r0, SSM_ROWS), :] = nr
                buf[ns + s, pl.ds(r0, SSM_ROWS), :] = ni
                new.append((nr, ni))
            return tuple(x[0] for x in new) + tuple(x[1] for x in new)

        init = tuple(st_ref[k, s] for s in range(2 * ns))
        final = lax.fori_loop(0, tt, step, init, unroll=8)
        for s in range(2 * ns):
            st_ref[k, s] = final[s]

        for a in range(SSM_HALVES):
            states = jnp.concatenate(
                [buf[s, pl.ds(a, nt, stride=SSM_HALVES), :] for s in range(2 * ns)], axis=1)
            y = jnp.dot(states.astype(BF16), wc_ref[a, k], preferred_element_type=F32)
            y = _gelu_tanh(y + d_ref[a, k] * block_input(a))
            for j in range(slabs_per_block):
                y_scr[a, k * slabs_per_block + j] = y[:, j * LANES:(j + 1) * LANES]

    for b in range(BATCH):
        for a in range(SSM_HALVES):
            for j in range(SSM_FEAT_SLABS):
                lane0 = a * SSM_HALF_FEATS + j * LANES
                ytok[b * tt:(b + 1) * tt, lane0:lane0 + LANES] = (
                    y_scr[a, j, pl.ds(b, tt, stride=BATCH), :].astype(BF16))

    y = ytok[...]
    z = jnp.dot(y, gw_ref[...], preferred_element_type=F32) + gb_ref[...]
    gate = bg_ref[...].reshape(nt, BRANCH_W).astype(F32)
    out = y.astype(F32) * _sigmoid(z) * _silu(gate)
    o_ref[...] = out.reshape(BATCH, tt, BRANCH_W).astype(o_ref.dtype)


def _s5_branch(proj, s5w, glu_w_bf, glu_b, layer, tt=SSM_TIME_BLOCK):
    wbu, wc, lre, lim, dvec = s5w
    nt = BATCH * tt
    w = BRANCH_W
    proj3 = proj.reshape(BATCH, SEQ, N_IN)
    gb3, gb_spec = _layer_vec(glu_b, layer)

    def whole(arr):
        nd = arr.ndim - 1
        return pl.BlockSpec((None,) + arr.shape[1:], lambda i: (layer,) + (0,) * nd)

    out = pl.pallas_call(
        functools.partial(_s5_kernel, tt=tt),
        grid=(SEQ // tt,),
        in_specs=[pl.BlockSpec((BATCH, tt, w), lambda i: (0, i, COL_B_X)),
                  pl.BlockSpec((BATCH, tt, w), lambda i: (0, i, COL_B_G)),
                  whole(wbu), whole(wc), whole(lre), whole(lim), whole(dvec),
                  whole(glu_w_bf), gb_spec],
        out_specs=pl.BlockSpec((BATCH, tt, w), lambda i: (0, i, 0)),
        out_shape=jax.ShapeDtypeStruct((BATCH, SEQ, w), BF16),
        scratch_shapes=[
            pltpu.VMEM((SSM_LANE_BLOCKS, 2 * SSM_STATE_SLABS, SSM_ROWS, LANES), F32),
            pltpu.VMEM((SSM_HALVES, SSM_FEAT_SLABS, nt, LANES), F32),
            pltpu.VMEM((2 * SSM_STATE_SLABS, SSM_ROWS * tt, LANES), F32),
            pltpu.VMEM((SSM_HALVES, SSM_FEAT_SLABS, nt, LANES), F32),
            pltpu.VMEM((nt, w), BF16),
        ],
        compiler_params=_params("arbitrary"),
        name="s5_branch",
    )(proj3, proj3, wbu, wc, lre, lim, dvec, glu_w_bf, gb3)
    return out.reshape(TOKENS, w)


def _s5_discretise(a_re, a_im, log_dt, b_re, b_im, c_re, c_im, d):
    nl = a_re.shape[0]
    lam_re = jnp.minimum(a_re, -EIG_CLIP)
    lam_im = a_im
    dt = jnp.exp(log_dt)[:, :, None]
    mag = jnp.exp(lam_re * dt)
    lb_re = mag * jnp.cos(lam_im * dt)
    lb_im = mag * jnp.sin(lam_im * dt)
    den = lam_re * lam_re + lam_im * lam_im
    q_re = ((lb_re - 1) * lam_re + lb_im * lam_im) / den
    q_im = (lb_im * lam_re - (lb_re - 1) * lam_im) / den
    bbar_re = q_re[..., None] * b_re - q_im[..., None] * b_im
    bbar_im = q_re[..., None] * b_im + q_im[..., None] * b_re
    eye = jnp.eye(SSM_BLOCK_GROUPS, dtype=F32)
    nh, nb, ng, p, h = SSM_HALVES, SSM_LANE_BLOCKS, SSM_BLOCK_GROUPS, SSM_STATE, SSM_GROUP_W

    def bu_weight(x):
        x = x.reshape(nl, nh, nb, ng, p, h)
        return jnp.einsum('lakiph,ij->lakihjp', x, eye).reshape(nl, nh, nb, ng * h, ng * p)

    wbu = jnp.concatenate([bu_weight(bbar_re), bu_weight(bbar_im)], axis=-1)

    def c_weight(x):
        x = x.reshape(nl, nh, nb, ng, h, p)
        return jnp.einsum('lakihp,ij->lakipjh', x, eye).reshape(nl, nh, nb, ng * p, ng * h)

    wc = jnp.concatenate([c_weight(c_re), -c_weight(c_im)], axis=-2)

    def lam_rows(x):
        x = x.reshape(nl, nh, nb, ng * p).transpose(0, 2, 1, 3)
        x = jnp.broadcast_to(x[:, :, None], (nl, nb, BATCH, nh, ng * p))
        return x.reshape(nl, nb, SSM_ROWS, ng * p)

    dvec = d.reshape(nl, nh, nb, 1, ng * h)
    return wbu.astype(BF16), wc.astype(BF16), lam_rows(lb_re), lam_rows(lb_im), dvec


def _kv_kernel(mem_ref, g_ref, wk_ref, wv_ref, k_ref, v_ref):
    mem_n = _rms(mem_ref[...], g_ref[...]).astype(BF16)
    k_ref[...] = jnp.dot(mem_n, wk_ref[...].astype(BF16),
                         preferred_element_type=F32).astype(k_ref.dtype)
    v_ref[...] = jnp.dot(mem_n, wv_ref[...].astype(BF16),
                         preferred_element_type=F32).astype(v_ref.dtype)


def _memory_kv(mem2d, mem_norm_g, xa_wk, xa_wv, tn=512):
    m, d = mem2d.shape
    nl = xa_wk.shape[0]
    w = BRANCH_W
    g3 = mem_norm_g.reshape(nl, 1, d)
    w_spec = pl.BlockSpec((None, d, tn), lambda l, j: (l, 0, j))
    o_spec = pl.BlockSpec((None, m, tn), lambda l, j: (l, 0, j))
    o_shape = jax.ShapeDtypeStruct((nl, m, w), BF16)
    return pl.pallas_call(
        _kv_kernel,
        grid=(nl, w // tn),
        in_specs=[pl.BlockSpec((m, d), lambda l, j: (0, 0)),
                  pl.BlockSpec((None, 1, d), lambda l, j: (l, 0, 0)),
                  w_spec, w_spec],
        out_specs=[o_spec, o_spec],
        out_shape=[o_shape, o_shape],
        compiler_params=_params("parallel", "parallel"),
        name="memory_kv",
    )(mem2d, g3, xa_wk, xa_wv)


def _xattn_kernel(q_ref, xg_ref, k_ref, v_ref, o_ref):
    scale = XA_HEAD_DIM ** -0.5
    for hd in range(XA_HEADS):
        lanes = slice(hd * XA_HEAD_DIM, (hd + 1) * XA_HEAD_DIM)
        s = lax.dot_general(q_ref[:, lanes], k_ref[:, lanes], (((1,), (1,)), ((), ())),
                            preferred_element_type=F32) * scale
        p = jnp.exp(s - jnp.max(s, axis=-1, keepdims=True))
        denom = jnp.sum(p, axis=-1, keepdims=True)
        o = jnp.dot(p.astype(BF16), v_ref[:, lanes], preferred_element_type=F32) / denom
        o = o * _silu(xg_ref[:, lanes].astype(F32))
        o_ref[:, lanes] = o.astype(o_ref.dtype)


def _cross_attention(proj, k_all, v_all, layer, tq=1024):
    w = BRANCH_W
    per_b = SEQ // tq
    kv_spec = pl.BlockSpec((None, N_MEM, w), lambda i: (layer, i // per_b, 0))
    return pl.pallas_call(
        _xattn_kernel,
        grid=(TOKENS // tq,),
        in_specs=[pl.BlockSpec((tq, w), lambda i: (i, COL_X_Q)),
                  pl.BlockSpec((tq, w), lambda i: (i, COL_X_G)),
                  kv_spec, kv_spec],
        out_specs=pl.BlockSpec((tq, w), lambda i: (i, 0)),
        out_shape=jax.ShapeDtypeStruct((TOKENS, w), BF16),
        compiler_params=_params("parallel"),
        name="cross_attention",
    )(proj, proj, k_all, v_all)


def _merge_kernel(ya_ref, yb_ref, yx_ref, ga_ref, gb_ref, gx_ref, wa_ref, wb_ref, wx_ref, o_ref):
    acc = _sigmoid(ga_ref[...].astype(F32)) * jnp.dot(
        ya_ref[...], wa_ref[...], preferred_element_type=F32)
    acc += _sigmoid(gb_ref[...].astype(F32)) * jnp.dot(
        yb_ref[...], wb_ref[...], preferred_element_type=F32)
    acc += _sigmoid(gx_ref[...].astype(F32)) * jnp.dot(
        yx_ref[...], wx_ref[...], preferred_element_type=F32)
    o_ref[...] = acc.astype(o_ref.dtype)


def _merge(y_a, y_b, y_x, proj, w_branch_bf, layer, tm=1024, tn=1024):
    w = BRANCH_W
    n_per_branch = D_MODEL // tn

    def y_spec():
        return pl.BlockSpec((tm, w), lambda n, i: (i, 0))

    def gate_spec(branch):
        return pl.BlockSpec((tm, tn), lambda n, i: (i, COL_GATES + branch * n_per_branch + n))

    def w_spec(branch):
        return pl.BlockSpec((None, None, w, tn), lambda n, i: (layer, branch, 0, n))

    return pl.pallas_call(
        _merge_kernel,
        grid=(n_per_branch, TOKENS // tm),
        in_specs=[y_spec(), y_spec(), y_spec(),
                  gate_spec(0), gate_spec(1), gate_spec(2),
                  w_spec(0), w_spec(1), w_spec(2)],
        out_specs=pl.BlockSpec((tm, tn), lambda n, i: (i, n)),
        out_shape=jax.ShapeDtypeStruct((TOKENS, D_MODEL), BF16),
        compiler_params=_params("parallel", "parallel"),
        name="branch_merge",
    )(y_a, y_b, y_x, proj, proj, proj, w_branch_bf, w_branch_bf, w_branch_bf)


def _out_kernel(m_ref, w_ref, x_ref, g_ref, gn_ref, o_ref, h_ref):
    out = jnp.dot(m_ref[...], w_ref[...], preferred_element_type=F32)
    x_new = x_ref[...] + _rms(out, g_ref[...])
    o_ref[...] = x_new
    if h_ref is not None:
        h_ref[...] = _rms(x_new, gn_ref[...]).astype(h_ref.dtype)


def _out_kernel_last(m_ref, w_ref, x_ref, g_ref, o_ref):
    _out_kernel(m_ref, w_ref, x_ref, g_ref, None, o_ref, None)


def _out_proj(merged, w_out_bf, x2d, post_g, pre_g, layer, tm=512):
    d = D_MODEL
    last = layer == DEPTH - 1
    pg3, pg_spec = _layer_vec(post_g, layer)
    row_spec = pl.BlockSpec((tm, d), lambda i: (i, 0))
    in_specs = [row_spec, pl.BlockSpec((None, d, d), lambda i: (layer, 0, 0)), row_spec, pg_spec]
    operands = [merged, w_out_bf, x2d, pg3]
    x_shape = jax.ShapeDtypeStruct((TOKENS, d), F32)
    if last:
        body, out_specs, out_shape = _out_kernel_last, row_spec, x_shape
    else:
        ng3, ng_spec = _layer_vec(pre_g, layer + 1)
        in_specs.append(ng_spec)
        operands.append(ng3)
        body = _out_kernel
        out_specs = [row_spec, row_spec]
        out_shape = [x_shape, jax.ShapeDtypeStruct((TOKENS, d), BF16)]
    res = pl.pallas_call(
        body,
        grid=(TOKENS // tm,),
        in_specs=in_specs,
        out_specs=out_specs,
        out_shape=out_shape,
        compiler_params=_params("parallel"),
        name="out_proj",
    )(*operands)
    return (res, None) if last else (res[0], res[1])


def kernel(x, mem, pre_norm_g, post_norm_g, mem_norm_g, w_in, sg_ln_g, sg_ln_b, sg_w, sg_b,
           ssm_a_re, ssm_a_im, ssm_log_dt, ssm_b_re, ssm_b_im, ssm_c_re, ssm_c_im, ssm_d,
           glu_w, glu_b, xa_wk, xa_wv, w_branch, w_out):
    x2d = x.reshape(TOKENS, D_MODEL)
    mem2d = mem.reshape(BATCH * N_MEM, D_MODEL)
    glu_w_bf = glu_w.astype(BF16)
    w_branch_bf = w_branch.astype(BF16)
    w_out_bf = w_out.astype(BF16)
    sg_bias = jnp.repeat(jnp.swapaxes(sg_b, 1, 2), CHUNK, axis=2)
    s5w = _s5_discretise(ssm_a_re, ssm_a_im, ssm_log_dt, ssm_b_re, ssm_b_im,
                         ssm_c_re, ssm_c_im, ssm_d)
    k_all, v_all = _memory_kv(mem2d, mem_norm_g, xa_wk, xa_wv)

    h = _rmsnorm(x2d, pre_norm_g, 0)
    for l in range(DEPTH):
        proj = _in_proj(h, w_in, l)
        y_a = _spatial_gating(proj, sg_ln_g, sg_ln_b, sg_w, sg_bias, l)
        y_b = _s5_branch(proj, s5w, glu_w_bf, glu_b, l)
        y_x = _cross_attention(proj, k_all, v_all, l)
        merged = _merge(y_a, y_b, y_x, proj, w_branch_bf, l)
        x2d, h = _out_proj(merged, w_out_bf, x2d, post_norm_g, pre_norm_g, l)
    return x2d.reshape(BATCH, SEQ, D_MODEL)
```

```python
import functools
import math

import jax
import jax.numpy as jnp
from jax import lax
from jax.experimental import pallas as pl
from jax.experimental.pallas import tpu as pltpu

D_MODEL = 2048
BATCH = 4
SEQ = 2048
DEPTH = 4
N_MEM = 256
BRANCH_W = 1024
N_BRANCH = 3
CHUNK = 128
SG_GROUPS = 8
SSM_GROUP_W = 16
SSM_GROUPS = BRANCH_W // SSM_GROUP_W
SSM_STATE = 64
XA_HEADS = 4
XA_HEAD_DIM = BRANCH_W // XA_HEADS
N_IN = 7 * BRANCH_W + N_BRANCH * D_MODEL
EPS = 1e-6
EIG_CLIP = 1e-4

TOKENS = BATCH * SEQ
F32 = jnp.float32
BF16 = jnp.bfloat16

V7X_VMEM_LIMIT_BYTES = 56 * 1024 * 1024
LANES = 128
SUBLANES = 8

COL_A_U, COL_A_V, COL_A_G, COL_B_X, COL_B_G, COL_X_Q, COL_X_G, COL_GATES = range(8)

SSM_HALVES = 2
SSM_HALF_GROUPS = SSM_GROUPS // SSM_HALVES
SSM_LANE_BLOCKS = 2
SSM_BLOCK_GROUPS = SSM_HALF_GROUPS // SSM_LANE_BLOCKS
SSM_BLOCK_FEATS = SSM_BLOCK_GROUPS * SSM_GROUP_W
SSM_BLOCK_STATES = SSM_BLOCK_GROUPS * SSM_STATE
SSM_HALF_FEATS = SSM_HALF_GROUPS * SSM_GROUP_W
SSM_ROWS = SSM_HALVES * BATCH
SSM_STATE_SLABS = SSM_BLOCK_STATES // LANES
SSM_FEAT_SLABS = SSM_HALF_FEATS // LANES
SSM_TIME_BLOCK = 128


def _params(*sem):
    return pltpu.CompilerParams(dimension_semantics=sem, vmem_limit_bytes=V7X_VMEM_LIMIT_BYTES)


def _sigmoid(x):
    return 1.0 / (1.0 + jnp.exp(-x))


def _silu(x):
    return x * _sigmoid(x)


def _gelu_tanh(x):
    c = math.sqrt(2.0 / math.pi)
    return 0.5 * x * (1.0 + jnp.tanh(c * (x + 0.044715 * (x * x * x))))


def _rms(x, g):
    ms = jnp.mean(x * x, axis=-1, keepdims=True)
    return x * lax.rsqrt(ms + EPS) * g


def _layer_vec(arr, layer):
    n = arr.shape[-1]
    spec = pl.BlockSpec((None, 1, n), lambda *_: (layer, 0, 0))
    return arr.reshape(arr.shape[0], 1, n), spec


def _rmsnorm_kernel(x_ref, g_ref, o_ref):
    o_ref[...] = _rms(x_ref[...], g_ref[...]).astype(o_ref.dtype)


def _rmsnorm(x2d, g_all, layer, tm=512):
    m, d = x2d.shape
    g3, g_spec = _layer_vec(g_all, layer)
    return pl.pallas_call(
        _rmsnorm_kernel,
        grid=(m // tm,),
        in_specs=[pl.BlockSpec((tm, d), lambda i: (i, 0)), g_spec],
        out_specs=pl.BlockSpec((tm, d), lambda i: (i, 0)),
        out_shape=jax.ShapeDtypeStruct((m, d), BF16),
        compiler_params=_params("parallel"),
        name="rmsnorm",
    )(x2d, g3)


def _in_proj_kernel(a_ref, w_ref, o_ref, wbf_ref):
    @pl.when(pl.program_id(1) == 0)
    def _():
        wbf_ref[...] = w_ref[...].astype(BF16)

    o_ref[...] = jnp.dot(a_ref[...], wbf_ref[...], preferred_element_type=F32).astype(o_ref.dtype)


def _in_proj(h, w_in, layer, tm=2048, tn=1024):
    m, k = h.shape
    n = w_in.shape[-1]
    return pl.pallas_call(
        _in_proj_kernel,
        grid=(n // tn, m // tm),
        in_specs=[pl.BlockSpec((tm, k), lambda j, i: (i, 0)),
                  pl.BlockSpec((None, k, tn), lambda j, i: (layer, 0, j))],
        out_specs=pl.BlockSpec((tm, tn), lambda j, i: (i, j)),
        out_shape=jax.ShapeDtypeStruct((m, n), BF16),
        scratch_shapes=[pltpu.VMEM((k, tn), BF16)],
        compiler_params=_params("parallel", "arbitrary"),
        name="in_proj",
    )(h, w_in)


def _sg_kernel(u_ref, v_ref, g_ref, lng_ref, lnb_ref, w_ref, bias_ref, o_ref, *, n_chunks):
    v = v_ref[...].astype(F32)
    mu = jnp.mean(v, axis=-1, keepdims=True)
    vc = v - mu
    var = jnp.mean(vc * vc, axis=-1, keepdims=True)
    vn = (vc * lax.rsqrt(var + EPS) * lng_ref[...] + lnb_ref[...]).astype(BF16)
    row = lax.broadcasted_iota(jnp.int32, (CHUNK, CHUNK), 0)
    col = lax.broadcasted_iota(jnp.int32, (CHUNK, CHUNK), 1)
    causal = col <= row
    for grp in range(SG_GROUPS):
        w = jnp.where(causal, w_ref[grp], 0.0).astype(BF16)
        lanes = slice(grp * CHUNK, (grp + 1) * CHUNK)
        for c in range(n_chunks):
            rows = slice(c * CHUNK, (c + 1) * CHUNK)
            z = jnp.dot(w, vn[rows, lanes], preferred_element_type=F32) + bias_ref[:, lanes]
            gate = g_ref[rows, lanes].astype(F32)
            y = u_ref[rows, lanes].astype(F32) * z * _silu(gate)
            o_ref[rows, lanes] = y.astype(o_ref.dtype)


def _spatial_gating(proj, ln_g, ln_b, sg_w, sg_bias, layer, tr=512):
    n_chunks = tr // CHUNK
    w = BRANCH_W
    g3, g_spec = _layer_vec(ln_g, layer)
    b3, b_spec = _layer_vec(ln_b, layer)
    return pl.pallas_call(
        functools.partial(_sg_kernel, n_chunks=n_chunks),
        grid=(TOKENS // tr,),
        in_specs=[pl.BlockSpec((tr, w), lambda i: (i, COL_A_U)),
                  pl.BlockSpec((tr, w), lambda i: (i, COL_A_V)),
                  pl.BlockSpec((tr, w), lambda i: (i, COL_A_G)),
                  g_spec, b_spec,
                  pl.BlockSpec((None, SG_GROUPS, CHUNK, CHUNK), lambda i: (layer, 0, 0, 0)),
                  pl.BlockSpec((None, CHUNK, w), lambda i: (layer, 0, 0))],
        out_specs=pl.BlockSpec((tr, w), lambda i: (i, 0)),
        out_shape=jax.ShapeDtypeStruct((TOKENS, w), BF16),
        compiler_params=_params("parallel"),
        name="spatial_gating",
    )(proj, proj, proj, g3, b3, sg_w, sg_bias)


def _s5_kernel(bx_ref, bg_ref, wbu_ref, wc_ref, lre_ref, lim_ref, d_ref, gw_ref, gb_ref, o_ref,
               st_ref, u_scr, buf0, buf1, y_scr, ytok, *, tt):
    bufs = (buf0, buf1)
    nt = BATCH * tt
    ns = SSM_STATE_SLABS

    @pl.when(pl.program_id(0) == 0)
    def _():
        st_ref[...] = jnp.zeros_like(st_ref)

    for b in range(BATCH):
        for a in range(SSM_HALVES):
            for j in range(SSM_FEAT_SLABS):
                lane0 = a * SSM_HALF_FEATS + j * LANES
                u_scr[a, j, pl.ds(b, tt, stride=BATCH), :] = (
                    bx_ref[b, :, lane0:lane0 + LANES].astype(F32))

    slabs_per_block = SSM_BLOCK_FEATS // LANES

    def block_input(a, k):
        return jnp.concatenate(
            [u_scr[a, k * slabs_per_block + j] for j in range(slabs_per_block)], axis=1)

    def project_in(k):
        for a in range(SSM_HALVES):
            bu = jnp.dot(block_input(a, k).astype(BF16), wbu_ref[a, k],
                         preferred_element_type=F32)
            for s in range(2 * ns):
                bufs[k][s, pl.ds(a, nt, stride=SSM_HALVES), :] = bu[:, s * LANES:(s + 1) * LANES]

    def scan(k):
        buf = bufs[k]
        lam_re = [lre_ref[k, :, s * LANES:(s + 1) * LANES] for s in range(ns)]
        lam_im = [lim_ref[k, :, s * LANES:(s + 1) * LANES] for s in range(ns)]

        def step(t, carry):
            r0 = pl.multiple_of(t * SSM_ROWS, SSM_ROWS)
            new = []
            for s in range(ns):
                sr, si = carry[s], carry[ns + s]
                nr = lam_re[s] * sr - lam_im[s] * si + buf[s, pl.ds(r0, SSM_ROWS), :]
                ni = lam_re[s] * si + lam_im[s] * sr + buf[ns + s, pl.ds(r0, SSM_ROWS), :]
                buf[s, pl.ds(r0, SSM_ROWS), :] = nr
                buf[ns + s, pl.ds(r0, SSM_ROWS), :] = ni
                new.append((nr, ni))
            return tuple(x[0] for x in new) + tuple(x[1] for x in new)

        init = tuple(st_ref[k, s] for s in range(2 * ns))
        final = lax.fori_loop(0, tt, step, init, unroll=True)
        for s in range(2 * ns):
            st_ref[k, s] = final[s]

    def project_out(k):
        for a in range(SSM_HALVES):
            states = jnp.concatenate(
                [bufs[k][s, pl.ds(a, nt, stride=SSM_HALVES), :] for s in range(2 * ns)], axis=1)
            y = jnp.dot(states.astype(BF16), wc_ref[a, k], preferred_element_type=F32)
            y = _gelu_tanh(y + d_ref[a, k] * block_input(a, k))
            for j in range(slabs_per_block):
                y_scr[a, k * slabs_per_block + j] = y[:, j * LANES:(j + 1) * LANES]

    project_in(0)
    project_in(1)
    scan(0)
    project_out(0)
    scan(1)
    project_out(1)

    for b in range(BATCH):
        for a in range(SSM_HALVES):
            for j in range(SSM_FEAT_SLABS):
                lane0 = a * SSM_HALF_FEATS + j * LANES
                ytok[b * tt:(b + 1) * tt, lane0:lane0 + LANES] = (
                    y_scr[a, j, pl.ds(b, tt, stride=BATCH), :].astype(BF16))

    y = ytok[...]
    z = jnp.dot(y, gw_ref[...], preferred_element_type=F32) + gb_ref[...]
    gate = bg_ref[...].reshape(nt, BRANCH_W).astype(F32)
    out = y.astype(F32) * _sigmoid(z) * _silu(gate)
    o_ref[...] = out.reshape(BATCH, tt, BRANCH_W).astype(o_ref.dtype)


def _s5_branch(proj, s5w, glu_w_bf, glu_b, layer, tt=SSM_TIME_BLOCK):
    wbu, wc, lre, lim, dvec = s5w
    nt = BATCH * tt
    w = BRANCH_W
    proj3 = proj.reshape(BATCH, SEQ, N_IN)
    gb3, gb_spec = _layer_vec(glu_b, layer)

    def whole(arr):
        nd = arr.ndim - 1
        return pl.BlockSpec((None,) + arr.shape[1:], lambda i: (layer,) + (0,) * nd,
                            pipeline_mode=pl.Buffered(1))

    out = pl.pallas_call(
        functools.partial(_s5_kernel, tt=tt),
        grid=(SEQ // tt,),
        in_specs=[pl.BlockSpec((BATCH, tt, w), lambda i: (0, i, COL_B_X)),
                  pl.BlockSpec((BATCH, tt, w), lambda i: (0, i, COL_B_G)),
                  whole(wbu), whole(wc), whole(lre), whole(lim), whole(dvec),
                  whole(glu_w_bf), gb_spec],
        out_specs=pl.BlockSpec((BATCH, tt, w), lambda i: (0, i, 0)),
        out_shape=jax.ShapeDtypeStruct((BATCH, SEQ, w), BF16),
        scratch_shapes=[
            pltpu.VMEM((SSM_LANE_BLOCKS, 2 * SSM_STATE_SLABS, SSM_ROWS, LANES), F32),
            pltpu.VMEM((SSM_HALVES, SSM_FEAT_SLABS, nt, LANES), F32),
            pltpu.VMEM((2 * SSM_STATE_SLABS, SSM_ROWS * tt, LANES), F32),
            pltpu.VMEM((2 * SSM_STATE_SLABS, SSM_ROWS * tt, LANES), F32),
            pltpu.VMEM((SSM_HALVES, SSM_FEAT_SLABS, nt, LANES), F32),
            pltpu.VMEM((nt, w), BF16),
        ],
        compiler_params=_params("arbitrary"),
        name="s5_branch",
    )(proj3, proj3, wbu, wc, lre, lim, dvec, glu_w_bf, gb3)
    return out.reshape(TOKENS, w)


def _s5_discretise(a_re, a_im, log_dt, b_re, b_im, c_re, c_im, d):
    nl = a_re.shape[0]
    lam_re = jnp.minimum(a_re, -EIG_CLIP)
    lam_im = a_im
    dt = jnp.exp(log_dt)[:, :, None]
    mag = jnp.exp(lam_re * dt)
    lb_re = mag * jnp.cos(lam_im * dt)
    lb_im = mag * jnp.sin(lam_im * dt)
    den = lam_re * lam_re + lam_im * lam_im
    q_re = ((lb_re - 1) * lam_re + lb_im * lam_im) / den
    q_im = (lb_im * lam_re - (lb_re - 1) * lam_im) / den
    bbar_re = q_re[..., None] * b_re - q_im[..., None] * b_im
    bbar_im = q_re[..., None] * b_im + q_im[..., None] * b_re
    nh, nb, ng, p, h = SSM_HALVES, SSM_LANE_BLOCKS, SSM_BLOCK_GROUPS, SSM_STATE, SSM_GROUP_W
    feat_group = lax.broadcasted_iota(jnp.int32, (ng * h, ng * p), 0) // h
    state_group = lax.broadcasted_iota(jnp.int32, (ng * h, ng * p), 1) // p
    diag = feat_group == state_group

    def bu_weight(x):
        x = x.reshape(nl, nh, nb, ng, p, h).transpose(0, 1, 2, 3, 5, 4)
        x = jnp.tile(x.reshape(nl, nh, nb, ng * h, p), (1, 1, 1, 1, ng))
        return jnp.where(diag, x, 0.0).astype(BF16)

    wbu = jnp.concatenate([bu_weight(bbar_re), bu_weight(bbar_im)], axis=-1)

    def c_weight(x):
        x = x.reshape(nl, nh, nb, ng, h, p).transpose(0, 1, 2, 3, 5, 4)
        x = jnp.tile(x.reshape(nl, nh, nb, ng * p, h), (1, 1, 1, 1, ng))
        return jnp.where(diag.T, x, 0.0).astype(BF16)

    wc = jnp.concatenate([c_weight(c_re), -c_weight(c_im)], axis=-2)

    def lam_rows(x):
        x = x.reshape(nl, nh, nb, ng * p).transpose(0, 2, 1, 3)
        x = jnp.broadcast_to(x[:, :, None], (nl, nb, BATCH, nh, ng * p))
        return x.reshape(nl, nb, SSM_ROWS, ng * p)

    dvec = d.reshape(nl, nh, nb, 1, ng * h)
    return wbu, wc, lam_rows(lb_re), lam_rows(lb_im), dvec


def _kv_kernel(mem_ref, g_ref, wk_ref, wv_ref, k_ref, v_ref):
    mem_n = _rms(mem_ref[...], g_ref[...]).astype(BF16)
    k_ref[...] = jnp.dot(mem_n, wk_ref[...].astype(BF16),
                         preferred_element_type=F32).astype(k_ref.dtype)
    v_ref[...] = jnp.dot(mem_n, wv_ref[...].astype(BF16),
                         preferred_element_type=F32).astype(v_ref.dtype)


def _memory_kv(mem2d, mem_norm_g, xa_wk, xa_wv, tn=512):
    m, d = mem2d.shape
    nl = xa_wk.shape[0]
    w = BRANCH_W
    g3 = mem_norm_g.reshape(nl, 1, d)
    w_spec = pl.BlockSpec((None, d, tn), lambda l, j: (l, 0, j))
    o_spec = pl.BlockSpec((None, m, tn), lambda l, j: (l, 0, j))
    o_shape = jax.ShapeDtypeStruct((nl, m, w), BF16)
    return pl.pallas_call(
        _kv_kernel,
        grid=(nl, w // tn),
        in_specs=[pl.BlockSpec((m, d), lambda l, j: (0, 0)),
                  pl.BlockSpec((None, 1, d), lambda l, j: (l, 0, 0)),
                  w_spec, w_spec],
        out_specs=[o_spec, o_spec],
        out_shape=[o_shape, o_shape],
        compiler_params=_params("parallel", "parallel"),
        name="memory_kv",
    )(mem2d, g3, xa_wk, xa_wv)


def _xattn_kernel(q_ref, xg_ref, k_ref, v_ref, o_ref):
    scale = XA_HEAD_DIM ** -0.5
    for hd in range(XA_HEADS):
        lanes = slice(hd * XA_HEAD_DIM, (hd + 1) * XA_HEAD_DIM)
        s = lax.dot_general(q_ref[:, lanes], k_ref[:, lanes], (((1,), (1,)), ((), ())),
                            preferred_element_type=F32) * scale
        p = jnp.exp(s - jnp.max(s, axis=-1, keepdims=True))
        denom = jnp.sum(p, axis=-1, keepdims=True)
        o = jnp.dot(p.astype(BF16), v_ref[:, lanes], preferred_element_type=F32) / denom
        o = o * _silu(xg_ref[:, lanes].astype(F32))
        o_ref[:, lanes] = o.astype(o_ref.dtype)


def _cross_attention(proj, k_all, v_all, layer, tq=1024):
    w = BRANCH_W
    per_b = SEQ // tq
    kv_spec = pl.BlockSpec((None, N_MEM, w), lambda i: (layer, i // per_b, 0))
    return pl.pallas_call(
        _xattn_kernel,
        grid=(TOKENS // tq,),
        in_specs=[pl.BlockSpec((tq, w), lambda i: (i, COL_X_Q)),
                  pl.BlockSpec((tq, w), lambda i: (i, COL_X_G)),
                  kv_spec, kv_spec],
        out_specs=pl.BlockSpec((tq, w), lambda i: (i, 0)),
        out_shape=jax.ShapeDtypeStruct((TOKENS, w), BF16),
        compiler_params=_params("parallel"),
        name="cross_attention",
    )(proj, proj, k_all, v_all)


def _merge_kernel(ya_ref, yb_ref, yx_ref, ga_ref, gb_ref, gx_ref, wa_ref, wb_ref, wx_ref, o_ref,
                  wbf_ref):
    @pl.when(pl.program_id(1) == 0)
    def _():
        for branch, w_ref in enumerate((wa_ref, wb_ref, wx_ref)):
            wbf_ref[branch] = w_ref[...].astype(BF16)

    acc = _sigmoid(ga_ref[...].astype(F32)) * jnp.dot(
        ya_ref[...], wbf_ref[0], preferred_element_type=F32)
    acc += _sigmoid(gb_ref[...].astype(F32)) * jnp.dot(
        yb_ref[...], wbf_ref[1], preferred_element_type=F32)
    acc += _sigmoid(gx_ref[...].astype(F32)) * jnp.dot(
        yx_ref[...], wbf_ref[2], preferred_element_type=F32)
    o_ref[...] = acc.astype(o_ref.dtype)


def _merge(y_a, y_b, y_x, proj, w_branch, layer, tm=1024, tn=1024):
    w = BRANCH_W
    n_per_branch = D_MODEL // tn

    def y_spec():
        return pl.BlockSpec((tm, w), lambda n, i: (i, 0))

    gate_col0 = COL_GATES * BRANCH_W // tn

    def gate_spec(branch):
        return pl.BlockSpec((tm, tn), lambda n, i: (i, gate_col0 + branch * n_per_branch + n))

    def w_spec(branch):
        return pl.BlockSpec((None, None, w, tn), lambda n, i: (layer, branch, 0, n),
                            pipeline_mode=pl.Buffered(1))

    return pl.pallas_call(
        _merge_kernel,
        grid=(n_per_branch, TOKENS // tm),
        in_specs=[y_spec(), y_spec(), y_spec(),
                  gate_spec(0), gate_spec(1), gate_spec(2),
                  w_spec(0), w_spec(1), w_spec(2)],
        out_specs=pl.BlockSpec((tm, tn), lambda n, i: (i, n)),
        out_shape=jax.ShapeDtypeStruct((TOKENS, D_MODEL), BF16),
        scratch_shapes=[pltpu.VMEM((N_BRANCH, w, tn), BF16)],
        compiler_params=_params("parallel", "arbitrary"),
        name="branch_merge",
    )(y_a, y_b, y_x, proj, proj, proj, w_branch, w_branch, w_branch)


def _out_kernel(m_ref, w_ref, x_ref, g_ref, gn_ref, o_ref, h_ref):
    out = jnp.dot(m_ref[...], w_ref[...], preferred_element_type=F32)
    x_new = x_ref[...] + _rms(out, g_ref[...])
    o_ref[...] = x_new
    if h_ref is not None:
        h_ref[...] = _rms(x_new, gn_ref[...]).astype(h_ref.dtype)


def _out_kernel_last(m_ref, w_ref, x_ref, g_ref, o_ref):
    _out_kernel(m_ref, w_ref, x_ref, g_ref, None, o_ref, None)


def _out_proj(merged, w_out_bf, x2d, post_g, pre_g, layer, tm=512):
    d = D_MODEL
    last = layer == DEPTH - 1
    pg3, pg_spec = _layer_vec(post_g, layer)
    row_spec = pl.BlockSpec((tm, d), lambda i: (i, 0))
    in_specs = [row_spec, pl.BlockSpec((None, d, d), lambda i: (layer, 0, 0)), row_spec, pg_spec]
    operands = [merged, w_out_bf, x2d, pg3]
    x_shape = jax.ShapeDtypeStruct((TOKENS, d), F32)
    if last:
        body, out_specs, out_shape = _out_kernel_last, row_spec, x_shape
    else:
        ng3, ng_spec = _layer_vec(pre_g, layer + 1)
        in_specs.append(ng_spec)
        operands.append(ng3)
        body = _out_kernel
        out_specs = [row_spec, row_spec]
        out_shape = [x_shape, jax.ShapeDtypeStruct((TOKENS, d), BF16)]
    res = pl.pallas_call(
        body,
        grid=(TOKENS // tm,),
        in_specs=in_specs,
        out_specs=out_specs,
        out_shape=out_shape,
        compiler_params=_params("parallel"),
        name="out_proj",
    )(*operands)
    return (res, None) if last else (res[0], res[1])


def kernel(x, mem, pre_norm_g, post_norm_g, mem_norm_g, w_in, sg_ln_g, sg_ln_b, sg_w, sg_b,
           ssm_a_re, ssm_a_im, ssm_log_dt, ssm_b_re, ssm_b_im, ssm_c_re, ssm_c_im, ssm_d,
           glu_w, glu_b, xa_wk, xa_wv, w_branch, w_out):
    x2d = x.reshape(TOKENS, D_MODEL)
    mem2d = mem.reshape(BATCH * N_MEM, D_MODEL)
    glu_w_bf = glu_w.astype(BF16)
    w_out_bf = w_out.astype(BF16)
    sg_bias = jnp.repeat(jnp.swapaxes(sg_b, 1, 2), CHUNK, axis=2)
    s5w = _s5_discretise(ssm_a_re, ssm_a_im, ssm_log_dt, ssm_b_re, ssm_b_im,
                         ssm_c_re, ssm_c_im, ssm_d)
    k_all, v_all = _memory_kv(mem2d, mem_norm_g, xa_wk, xa_wv)

    h = _rmsnorm(x2d, pre_norm_g, 0)
    for l in range(DEPTH):
        proj = _in_proj(h, w_in, l)
        y_a = _spatial_gating(proj, sg_ln_g, sg_ln_b, sg_w, sg_bias, l)
        y_b = _s5_branch(proj, s5w, glu_w_bf, glu_b, l)
        y_x = _cross_attention(proj, k_all, v_all, l)
        merged = _merge(y_a, y_b, y_x, proj, w_branch, l)
        x2d, h = _out_proj(merged, w_out_bf, x2d, post_norm_g, pre_norm_g, l)
    return x2d.reshape(BATCH, SEQ, D_MODEL)
```

```python
import functools
import math

import jax
import jax.numpy as jnp
from jax import lax
from jax.experimental import pallas as pl
from jax.experimental.pallas import tpu as pltpu

D_MODEL = 2048
BATCH = 4
SEQ = 2048
DEPTH = 4
N_MEM = 256
BRANCH_W = 1024
N_BRANCH = 3
CHUNK = 128
SG_GROUPS = 8
SSM_GROUP_W = 16
SSM_GROUPS = BRANCH_W // SSM_GROUP_W
SSM_STATE = 64
XA_HEADS = 4
XA_HEAD_DIM = BRANCH_W // XA_HEADS
N_IN = 7 * BRANCH_W + N_BRANCH * D_MODEL
EPS = 1e-6
EIG_CLIP = 1e-4

TOKENS = BATCH * SEQ
F32 = jnp.float32
BF16 = jnp.bfloat16

V7X_VMEM_LIMIT_BYTES = 56 * 1024 * 1024
LANES = 128
SUBLANES = 8

COL_A_U, COL_A_V, COL_A_G, COL_B_X, COL_B_G, COL_X_Q, COL_X_G, COL_GATES = range(8)

SSM_HALVES = 2
SSM_HALF_GROUPS = SSM_GROUPS // SSM_HALVES
SSM_LANE_BLOCKS = 2
SSM_BLOCK_GROUPS = SSM_HALF_GROUPS // SSM_LANE_BLOCKS
SSM_BLOCK_FEATS = SSM_BLOCK_GROUPS * SSM_GROUP_W
SSM_BLOCK_STATES = SSM_BLOCK_GROUPS * SSM_STATE
SSM_HALF_FEATS = SSM_HALF_GROUPS * SSM_GROUP_W
SSM_ROWS = SSM_HALVES * BATCH
SSM_STATE_SLABS = SSM_BLOCK_STATES // LANES
SSM_FEAT_SLABS = SSM_HALF_FEATS // LANES
SSM_TIME_BLOCK = 128


def _params(*sem):
    return pltpu.CompilerParams(dimension_semantics=sem, vmem_limit_bytes=V7X_VMEM_LIMIT_BYTES)


def _sigmoid(x):
    return 1.0 / (1.0 + jnp.exp(-x))


def _silu(x):
    return x * _sigmoid(x)


def _gelu_tanh(x):
    c = math.sqrt(2.0 / math.pi)
    return 0.5 * x * (1.0 + jnp.tanh(c * (x + 0.044715 * (x * x * x))))


def _rms(x, g):
    ms = jnp.mean(x * x, axis=-1, keepdims=True)
    return x * lax.rsqrt(ms + EPS) * g


def _layer_vec(arr, layer):
    n = arr.shape[-1]
    spec = pl.BlockSpec((None, 1, n), lambda *_: (layer, 0, 0))
    return arr.reshape(arr.shape[0], 1, n), spec


def _rmsnorm_kernel(x_ref, g_ref, o_ref):
    o_ref[...] = _rms(x_ref[...], g_ref[...]).astype(o_ref.dtype)


def _rmsnorm(x2d, g_all, layer, tm=512):
    m, d = x2d.shape
    g3, g_spec = _layer_vec(g_all, layer)
    return pl.pallas_call(
        _rmsnorm_kernel,
        grid=(m // tm,),
        in_specs=[pl.BlockSpec((tm, d), lambda i: (i, 0)), g_spec],
        out_specs=pl.BlockSpec((tm, d), lambda i: (i, 0)),
        out_shape=jax.ShapeDtypeStruct((m, d), BF16),
        compiler_params=_params("parallel"),
        name="rmsnorm",
    )(x2d, g3)


def _in_proj_kernel(a_ref, w_ref, o_ref, wbf_ref):
    @pl.when(pl.program_id(1) == 0)
    def _():
        wbf_ref[...] = w_ref[...].astype(BF16)

    o_ref[...] = jnp.dot(a_ref[...], wbf_ref[...], preferred_element_type=F32).astype(o_ref.dtype)


def _in_proj(h, w_in, layer, tm=2048, tn=1024):
    m, k = h.shape
    n = w_in.shape[-1]
    return pl.pallas_call(
        _in_proj_kernel,
        grid=(n // tn, m // tm),
        in_specs=[pl.BlockSpec((tm, k), lambda j, i: (i, 0)),
                  pl.BlockSpec((None, k, tn), lambda j, i: (layer, 0, j))],
        out_specs=pl.BlockSpec((tm, tn), lambda j, i: (i, j)),
        out_shape=jax.ShapeDtypeStruct((m, n), BF16),
        scratch_shapes=[pltpu.VMEM((k, tn), BF16)],
        compiler_params=_params("parallel", "arbitrary"),
        name="in_proj",
    )(h, w_in)


def _sg_kernel(u_ref, v_ref, g_ref, lng_ref, lnb_ref, w_ref, bias_ref, wbr_ref, wout_ref,
               o_ref, wbr_bf_ref, wout_bf_ref, *, n_chunks):
    wbr_bf_ref[...] = wbr_ref[...].astype(BF16)
    wout_bf_ref[...] = wout_ref[...].astype(BF16)

    v = v_ref[...].astype(F32)
    mu = jnp.mean(v, axis=-1, keepdims=True)
    vc = v - mu
    var = jnp.mean(vc * vc, axis=-1, keepdims=True)
    vn = (vc * lax.rsqrt(var + EPS) * lng_ref[...] + lnb_ref[...]).astype(BF16)
    row = lax.broadcasted_iota(jnp.int32, (CHUNK, CHUNK), 0)
    col = lax.broadcasted_iota(jnp.int32, (CHUNK, CHUNK), 1)
    causal = col <= row
    for grp in range(SG_GROUPS):
        w = jnp.where(causal, w_ref[grp], 0.0).astype(BF16)
        lanes = slice(grp * CHUNK, (grp + 1) * CHUNK)
        for c in range(n_chunks):
            rows = slice(c * CHUNK, (c + 1) * CHUNK)
            z = jnp.dot(w, vn[rows, lanes], preferred_element_type=F32) + bias_ref[:, lanes]
            gate = g_ref[rows, lanes].astype(F32)
            y = u_ref[rows, lanes].astype(F32) * z * _silu(gate)
            o_ref[rows, lanes] = y.astype(o_ref.dtype)


def _spatial_gating(proj, ln_g, ln_b, sg_w, sg_bias, w_branch, w_out, layer, tr=512):
    n_chunks = tr // CHUNK
    n_steps = TOKENS // tr
    w = BRANCH_W
    d = D_MODEL
    g3, g_spec = _layer_vec(ln_g, layer)
    b3, b_spec = _layer_vec(ln_b, layer)
    wbr_rows = N_BRANCH * w // n_steps
    wout_rows = d // n_steps
    y_a, wbr_bf, wout_bf = pl.pallas_call(
        functools.partial(_sg_kernel, n_chunks=n_chunks),
        grid=(n_steps,),
        in_specs=[pl.BlockSpec((tr, w), lambda i: (i, COL_A_U)),
                  pl.BlockSpec((tr, w), lambda i: (i, COL_A_V)),
                  pl.BlockSpec((tr, w), lambda i: (i, COL_A_G)),
                  g_spec, b_spec,
                  pl.BlockSpec((None, SG_GROUPS, CHUNK, CHUNK), lambda i: (layer, 0, 0, 0)),
                  pl.BlockSpec((None, CHUNK, w), lambda i: (layer, 0, 0)),
                  pl.BlockSpec((None, wbr_rows, d), lambda i: (layer, i, 0)),
                  pl.BlockSpec((None, wout_rows, d), lambda i: (layer, i, 0))],
        out_specs=[pl.BlockSpec((tr, w), lambda i: (i, 0)),
                   pl.BlockSpec((wbr_rows, d), lambda i: (i, 0)),
                   pl.BlockSpec((wout_rows, d), lambda i: (i, 0))],
        out_shape=[jax.ShapeDtypeStruct((TOKENS, w), BF16),
                   jax.ShapeDtypeStruct((N_BRANCH * w, d), BF16),
                   jax.ShapeDtypeStruct((d, d), BF16)],
        compiler_params=_params("parallel"),
        name="spatial_gating",
    )(proj, proj, proj, g3, b3, sg_w, sg_bias,
      w_branch.reshape(DEPTH, N_BRANCH * w, d), w_out)
    return y_a, wbr_bf.reshape(N_BRANCH, w, d), wout_bf


def _s5_kernel(bx_ref, bg_ref, wbu_ref, wc_ref, lre_ref, lim_ref, d_ref, gw_ref, gb_ref, o_ref,
               st_ref, u_scr, buf0, buf1, y_scr, ytok, *, tt):
    bufs = (buf0, buf1)
    nt = BATCH * tt
    ns = SSM_STATE_SLABS

    @pl.when(pl.program_id(0) == 0)
    def _():
        st_ref[...] = jnp.zeros_like(st_ref)

    for b in range(BATCH):
        for a in range(SSM_HALVES):
            for j in range(SSM_FEAT_SLABS):
                lane0 = a * SSM_HALF_FEATS + j * LANES
                u_scr[a, j, pl.ds(b, tt, stride=BATCH), :] = (
                    bx_ref[b, :, lane0:lane0 + LANES].astype(F32))

    slabs_per_block = SSM_BLOCK_FEATS // LANES

    def block_input(a, k):
        return jnp.concatenate(
            [u_scr[a, k * slabs_per_block + j] for j in range(slabs_per_block)], axis=1)

    def project_in(k):
        for a in range(SSM_HALVES):
            bu = jnp.dot(block_input(a, k).astype(BF16), wbu_ref[a, k],
                         preferred_element_type=F32)
            for s in range(2 * ns):
                bufs[k][s, pl.ds(a, nt, stride=SSM_HALVES), :] = bu[:, s * LANES:(s + 1) * LANES]

    def scan(k):
        buf = bufs[k]
        lam_re = [lre_ref[k, :, s * LANES:(s + 1) * LANES] for s in range(ns)]
        lam_im = [lim_ref[k, :, s * LANES:(s + 1) * LANES] for s in range(ns)]

        def step(t, carry):
            r0 = pl.multiple_of(t * SSM_ROWS, SSM_ROWS)
            new = []
            for s in range(ns):
                sr, si = carry[s], carry[ns + s]
                nr = lam_re[s] * sr - lam_im[s] * si + buf[s, pl.ds(r0, SSM_ROWS), :]
                ni = lam_re[s] * si + lam_im[s] * sr + buf[ns + s, pl.ds(r0, SSM_ROWS), :]
                buf[s, pl.ds(r0, SSM_ROWS), :] = nr
                buf[ns + s, pl.ds(r0, SSM_ROWS), :] = ni
                new.append((nr, ni))
            return tuple(x[0] for x in new) + tuple(x[1] for x in new)

        init = tuple(st_ref[k, s] for s in range(2 * ns))
        final = lax.fori_loop(0, tt, step, init, unroll=True)
        for s in range(2 * ns):
            st_ref[k, s] = final[s]

    def project_out(k):
        for a in range(SSM_HALVES):
            states = jnp.concatenate(
                [bufs[k][s, pl.ds(a, nt, stride=SSM_HALVES), :] for s in range(2 * ns)], axis=1)
            y = jnp.dot(states.astype(BF16), wc_ref[a, k], preferred_element_type=F32)
            y = _gelu_tanh(y + d_ref[a, k] * block_input(a, k))
            for j in range(slabs_per_block):
                y_scr[a, k * slabs_per_block + j] = y[:, j * LANES:(j + 1) * LANES]

    project_in(0)
    project_in(1)
    scan(0)
    project_out(0)
    scan(1)
    project_out(1)

    for b in range(BATCH):
        for a in range(SSM_HALVES):
            for j in range(SSM_FEAT_SLABS):
                lane0 = a * SSM_HALF_FEATS + j * LANES
                ytok[b * tt:(b + 1) * tt, lane0:lane0 + LANES] = (
                    y_scr[a, j, pl.ds(b, tt, stride=BATCH), :].astype(BF16))

    y = ytok[...]
    z = jnp.dot(y, gw_ref[...], preferred_element_type=F32) + gb_ref[...]
    gate = bg_ref[...].reshape(nt, BRANCH_W).astype(F32)
    out = y.astype(F32) * _sigmoid(z) * _silu(gate)
    o_ref[...] = out.reshape(BATCH, tt, BRANCH_W).astype(o_ref.dtype)


def _s5_branch(proj, s5w, glu_w_bf, glu_b, layer, tt=SSM_TIME_BLOCK):
    wbu, wc, lre, lim, dvec = s5w
    nt = BATCH * tt
    w = BRANCH_W
    proj3 = proj.reshape(BATCH, SEQ, N_IN)
    gb3, gb_spec = _layer_vec(glu_b, layer)

    def whole(arr):
        nd = arr.ndim - 1
        return pl.BlockSpec((None,) + arr.shape[1:], lambda i: (layer,) + (0,) * nd,
                            pipeline_mode=pl.Buffered(1))

    out = pl.pallas_call(
        functools.partial(_s5_kernel, tt=tt),
        grid=(SEQ // tt,),
        in_specs=[pl.BlockSpec((BATCH, tt, w), lambda i: (0, i, COL_B_X)),
                  pl.BlockSpec((BATCH, tt, w), lambda i: (0, i, COL_B_G)),
                  whole(wbu), whole(wc), whole(lre), whole(lim), whole(dvec),
                  whole(glu_w_bf), gb_spec],
        out_specs=pl.BlockSpec((BATCH, tt, w), lambda i: (0, i, 0)),
        out_shape=jax.ShapeDtypeStruct((BATCH, SEQ, w), BF16),
        scratch_shapes=[
            pltpu.VMEM((SSM_LANE_BLOCKS, 2 * SSM_STATE_SLABS, SSM_ROWS, LANES), F32),
            pltpu.VMEM((SSM_HALVES, SSM_FEAT_SLABS, nt, LANES), F32),
            pltpu.VMEM((2 * SSM_STATE_SLABS, SSM_ROWS * tt, LANES), F32),
            pltpu.VMEM((2 * SSM_STATE_SLABS, SSM_ROWS * tt, LANES), F32),
            pltpu.VMEM((SSM_HALVES, SSM_FEAT_SLABS, nt, LANES), F32),
            pltpu.VMEM((nt, w), BF16),
        ],
        compiler_params=_params("arbitrary"),
        name="s5_branch",
    )(proj3, proj3, wbu, wc, lre, lim, dvec, glu_w_bf, gb3)
    return out.reshape(TOKENS, w)


def _s5_discretise(a_re, a_im, log_dt, b_re, b_im, c_re, c_im, d):
    nl = a_re.shape[0]
    lam_re = jnp.minimum(a_re, -EIG_CLIP)
    lam_im = a_im
    dt = jnp.exp(log_dt)[:, :, None]
    mag = jnp.exp(lam_re * dt)
    lb_re = mag * jnp.cos(lam_im * dt)
    lb_im = mag * jnp.sin(lam_im * dt)
    den = lam_re * lam_re + lam_im * lam_im
    q_re = ((lb_re - 1) * lam_re + lb_im * lam_im) / den
    q_im = (lb_im * lam_re - (lb_re - 1) * lam_im) / den
    bbar_re = q_re[..., None] * b_re - q_im[..., None] * b_im
    bbar_im = q_re[..., None] * b_im + q_im[..., None] * b_re
    nh, nb, ng, p, h = SSM_HALVES, SSM_LANE_BLOCKS, SSM_BLOCK_GROUPS, SSM_STATE, SSM_GROUP_W
    feat_group = lax.broadcasted_iota(jnp.int32, (ng * h, ng * p), 0) // h
    state_group = lax.broadcasted_iota(jnp.int32, (ng * h, ng * p), 1) // p
    diag = feat_group == state_group

    def bu_weight(x):
        x = x.astype(BF16).reshape(nl, nh, nb, ng, p, h).transpose(0, 1, 2, 3, 5, 4)
        x = jnp.tile(x.reshape(nl, nh, nb, ng * h, p), (1, 1, 1, 1, ng))
        return jnp.where(diag, x, 0.0)

    wbu = jnp.concatenate([bu_weight(bbar_re), bu_weight(bbar_im)], axis=-1)

    def c_weight(x):
        x = x.astype(BF16).reshape(nl, nh, nb, ng, h, p).transpose(0, 1, 2, 3, 5, 4)
        x = jnp.tile(x.reshape(nl, nh, nb, ng * p, h), (1, 1, 1, 1, ng))
        return jnp.where(diag.T, x, 0.0)

    wc = jnp.concatenate([c_weight(c_re), -c_weight(c_im)], axis=-2)

    def lam_rows(x):
        x = x.reshape(nl, nh, nb, ng * p).transpose(0, 2, 1, 3)
        x = jnp.broadcast_to(x[:, :, None], (nl, nb, BATCH, nh, ng * p))
        return x.reshape(nl, nb, SSM_ROWS, ng * p)

    dvec = d.reshape(nl, nh, nb, 1, ng * h)
    return wbu, wc, lam_rows(lb_re), lam_rows(lb_im), dvec


def _kv_kernel(mem_ref, g_ref, wk_ref, wv_ref, k_ref, v_ref):
    mem_n = _rms(mem_ref[...], g_ref[...]).astype(BF16)
    k_ref[...] = jnp.dot(mem_n, wk_ref[...].astype(BF16),
                         preferred_element_type=F32).astype(k_ref.dtype)
    v_ref[...] = jnp.dot(mem_n, wv_ref[...].astype(BF16),
                         preferred_element_type=F32).astype(v_ref.dtype)


def _memory_kv(mem2d, mem_norm_g, xa_wk, xa_wv, tn=512):
    m, d = mem2d.shape
    nl = xa_wk.shape[0]
    w = BRANCH_W
    g3 = mem_norm_g.reshape(nl, 1, d)
    w_spec = pl.BlockSpec((None, d, tn), lambda l, j: (l, 0, j))
    o_spec = pl.BlockSpec((None, m, tn), lambda l, j: (l, 0, j))
    o_shape = jax.ShapeDtypeStruct((nl, m, w), BF16)
    return pl.pallas_call(
        _kv_kernel,
        grid=(nl, w // tn),
        in_specs=[pl.BlockSpec((m, d), lambda l, j: (0, 0)),
                  pl.BlockSpec((None, 1, d), lambda l, j: (l, 0, 0)),
                  w_spec, w_spec],
        out_specs=[o_spec, o_spec],
        out_shape=[o_shape, o_shape],
        compiler_params=_params("parallel", "parallel"),
        name="memory_kv",
    )(mem2d, g3, xa_wk, xa_wv)


def _xattn_kernel(q_ref, xg_ref, k_ref, v_ref, o_ref):
    scale = XA_HEAD_DIM ** -0.5
    for hd in range(XA_HEADS):
        lanes = slice(hd * XA_HEAD_DIM, (hd + 1) * XA_HEAD_DIM)
        s = lax.dot_general(q_ref[:, lanes], k_ref[:, lanes], (((1,), (1,)), ((), ())),
                            preferred_element_type=F32) * scale
        p = jnp.exp(s - jnp.max(s, axis=-1, keepdims=True))
        denom = jnp.sum(p, axis=-1, keepdims=True)
        o = jnp.dot(p.astype(BF16), v_ref[:, lanes], preferred_element_type=F32) / denom
        o = o * _silu(xg_ref[:, lanes].astype(F32))
        o_ref[:, lanes] = o.astype(o_ref.dtype)


def _cross_attention(proj, k_all, v_all, layer, tq=1024):
    w = BRANCH_W
    per_b = SEQ // tq
    kv_spec = pl.BlockSpec((None, N_MEM, w), lambda i: (layer, i // per_b, 0))
    return pl.pallas_call(
        _xattn_kernel,
        grid=(TOKENS // tq,),
        in_specs=[pl.BlockSpec((tq, w), lambda i: (i, COL_X_Q)),
                  pl.BlockSpec((tq, w), lambda i: (i, COL_X_G)),
                  kv_spec, kv_spec],
        out_specs=pl.BlockSpec((tq, w), lambda i: (i, 0)),
        out_shape=jax.ShapeDtypeStruct((TOKENS, w), BF16),
        compiler_params=_params("parallel"),
        name="cross_attention",
    )(proj, proj, k_all, v_all)


def _merge_kernel(ya_ref, yb_ref, yx_ref, ga_ref, gb_ref, gx_ref, wa_ref, wb_ref, wx_ref, o_ref):
    acc = _sigmoid(ga_ref[...].astype(F32)) * jnp.dot(
        ya_ref[...], wa_ref[...], preferred_element_type=F32)
    acc += _sigmoid(gb_ref[...].astype(F32)) * jnp.dot(
        yb_ref[...], wb_ref[...], preferred_element_type=F32)
    acc += _sigmoid(gx_ref[...].astype(F32)) * jnp.dot(
        yx_ref[...], wx_ref[...], preferred_element_type=F32)
    o_ref[...] = acc.astype(o_ref.dtype)


def _merge(y_a, y_b, y_x, proj, w_branch_bf, tm=1024, tn=1024):
    w = BRANCH_W
    n_per_branch = D_MODEL // tn

    def y_spec():
        return pl.BlockSpec((tm, w), lambda n, i: (i, 0))

    gate_col0 = COL_GATES * BRANCH_W // tn

    def gate_spec(branch):
        return pl.BlockSpec((tm, tn), lambda n, i: (i, gate_col0 + branch * n_per_branch + n))

    def w_spec(branch):
        return pl.BlockSpec((None, w, tn), lambda n, i: (branch, 0, n))

    return pl.pallas_call(
        _merge_kernel,
        grid=(n_per_branch, TOKENS // tm),
        in_specs=[y_spec(), y_spec(), y_spec(),
                  gate_spec(0), gate_spec(1), gate_spec(2),
                  w_spec(0), w_spec(1), w_spec(2)],
        out_specs=pl.BlockSpec((tm, tn), lambda n, i: (i, n)),
        out_shape=jax.ShapeDtypeStruct((TOKENS, D_MODEL), BF16),
        compiler_params=_params("parallel", "parallel"),
        name="branch_merge",
    )(y_a, y_b, y_x, proj, proj, proj, w_branch_bf, w_branch_bf, w_branch_bf)


def _out_kernel(m_ref, w_ref, x_ref, g_ref, gn_ref, o_ref, h_ref):
    out = jnp.dot(m_ref[...], w_ref[...], preferred_element_type=F32)
    x_new = x_ref[...] + _rms(out, g_ref[...])
    o_ref[...] = x_new
    if h_ref is not None:
        h_ref[...] = _rms(x_new, gn_ref[...]).astype(h_ref.dtype)


def _out_kernel_last(m_ref, w_ref, x_ref, g_ref, o_ref):
    _out_kernel(m_ref, w_ref, x_ref, g_ref, None, o_ref, None)


def _out_proj(merged, w_out_bf, x2d, post_g, pre_g, layer, tm=512):
    d = D_MODEL
    last = layer == DEPTH - 1
    pg3, pg_spec = _layer_vec(post_g, layer)
    row_spec = pl.BlockSpec((tm, d), lambda i: (i, 0))
    in_specs = [row_spec, pl.BlockSpec((d, d), lambda i: (0, 0)), row_spec, pg_spec]
    operands = [merged, w_out_bf, x2d, pg3]
    x_shape = jax.ShapeDtypeStruct((TOKENS, d), F32)
    if last:
        body, out_specs, out_shape = _out_kernel_last, row_spec, x_shape
    else:
        ng3, ng_spec = _layer_vec(pre_g, layer + 1)
        in_specs.append(ng_spec)
        operands.append(ng3)
        body = _out_kernel
        out_specs = [row_spec, row_spec]
        out_shape = [x_shape, jax.ShapeDtypeStruct((TOKENS, d), BF16)]
    res = pl.pallas_call(
        body,
        grid=(TOKENS // tm,),
        in_specs=in_specs,
        out_specs=out_specs,
        out_shape=out_shape,
        compiler_params=_params("parallel"),
        name="out_proj",
    )(*operands)
    return (res, None) if last else (res[0], res[1])


def kernel(x, mem, pre_norm_g, post_norm_g, mem_norm_g, w_in, sg_ln_g, sg_ln_b, sg_w, sg_b,
           ssm_a_re, ssm_a_im, ssm_log_dt, ssm_b_re, ssm_b_im, ssm_c_re, ssm_c_im, ssm_d,
           glu_w, glu_b, xa_wk, xa_wv, w_branch, w_out):
    x2d = x.reshape(TOKENS, D_MODEL)
    mem2d = mem.reshape(BATCH * N_MEM, D_MODEL)
    glu_w_bf = glu_w.astype(BF16)
    sg_bias = jnp.repeat(jnp.swapaxes(sg_b, 1, 2), CHUNK, axis=2)
    s5w = _s5_discretise(ssm_a_re, ssm_a_im, ssm_log_dt, ssm_b_re, ssm_b_im,
                         ssm_c_re, ssm_c_im, ssm_d)
    k_all, v_all = _memory_kv(mem2d, mem_norm_g, xa_wk, xa_wv)

    h = _rmsnorm(x2d, pre_norm_g, 0)
    for l in range(DEPTH):
        proj = _in_proj(h, w_in, l)
        y_a, w_branch_bf, w_out_bf = _spatial_gating(
            proj, sg_ln_g, sg_ln_b, sg_w, sg_bias, w_branch, w_out, l)
        y_b = _s5_branch(proj, s5w, glu_w_bf, glu_b, l)
        y_x = _cross_attention(proj, k_all, v_all, l)
        merged = _merge(y_a, y_b, y_x, proj, w_branch_bf)
        x2d, h = _out_proj(merged, w_out_bf, x2d, post_norm_g, pre_norm_g, l)
    return x2d.reshape(BATCH, SEQ, D_MODEL)
```
